```python
import jax, jax.numpy as jnp
from jax import lax
import numpy as np

D_MODEL = 4096
BATCH = 2
SEQ = 8192
DEPTH = 4

N_MIXERS = 2
N_LAYERS_A = (DEPTH + 1) // 2
N_LAYERS_B = DEPTH // 2
D_FF = 7 * D_MODEL // 4
MLA_HEADS = 32
QK_NOPE = 128
QK_ROPE = 64
V_HEAD = 128
Q_LORA = D_MODEL // 4
KV_LORA = D_MODEL // 8
ROPE_THETA = 10000.0
Q_BLOCK = 128
LRU_WIDTH = D_MODEL
LRU_BLOCKS = 16
LRU_BLOCK = LRU_WIDTH // LRU_BLOCKS
CONV_WIDTH = 4
LRU_C = 8.0
NORM_EPS = 1e-6

kernel_name = "hybrid_mla_rglru_macaron_trunk"


def rms_norm(x, g):
    x32 = x.astype(jnp.float32)
    y = x32 * lax.rsqrt(jnp.mean(x32 * x32, axis=-1, keepdims=True) + NORM_EPS)
    return (y * g.astype(jnp.float32)).astype(x.dtype)


def swiglu_ffn(x, w_in, w_out):
    gate, up = jnp.split(x @ w_in, 2, axis=-1)
    return (jax.nn.silu(gate) * up) @ w_out


def rope(x, positions):
    half = x.shape[-1] // 2
    inv_freq = ROPE_THETA ** (-jnp.arange(half, dtype=jnp.float32) / half)
    ang = positions.astype(jnp.float32)[..., None] * inv_freq
    ang = ang.reshape(ang.shape[:2] + (1,) * (x.ndim - 3) + (half,))
    cos, sin = jnp.cos(ang), jnp.sin(ang)
    x1 = x[..., :half].astype(jnp.float32)
    x2 = x[..., half:].astype(jnp.float32)
    return jnp.concatenate([x1 * cos - x2 * sin, x2 * cos + x1 * sin], axis=-1).astype(x.dtype)


def mla_mixer(u, positions, w_in, q_norm, kv_norm, w_uq, w_ukv, w_o):
    B, S, _ = u.shape
    c = u @ w_in
    c_q, c_kv, k_rope = jnp.split(c, [Q_LORA, Q_LORA + KV_LORA], axis=-1)
    q = (rms_norm(c_q, q_norm) @ w_uq).reshape(B, S, MLA_HEADS, QK_NOPE + QK_ROPE)
    q_nope, q_rope = jnp.split(q, [QK_NOPE], axis=-1)
    q_rope = rope(q_rope, positions)
    kv = (rms_norm(c_kv, kv_norm) @ w_ukv).reshape(B, S, MLA_HEADS, QK_NOPE + V_HEAD)
    k_nope, v = jnp.split(kv, [QK_NOPE], axis=-1)
    k_rope = rope(k_rope, positions)
    scale = (QK_NOPE + QK_ROPE) ** -0.5
    n_blk = S // Q_BLOCK
    qn_blocks = q_nope.reshape(B, n_blk, Q_BLOCK, MLA_HEADS, QK_NOPE).transpose(1, 0, 2, 3, 4)
    qr_blocks = q_rope.reshape(B, n_blk, Q_BLOCK, MLA_HEADS, QK_ROPE).transpose(1, 0, 2, 3, 4)
    starts = jnp.arange(n_blk, dtype=jnp.int32) * Q_BLOCK
    key_pos = jnp.arange(S, dtype=jnp.int32)
    neg = jnp.finfo(jnp.float32).min

    def attend(args):
        qn, qr, start = args
        s = jnp.einsum('bqhd,bkhd->bhqk', qn, k_nope, preferred_element_type=jnp.float32)
        s = s + jnp.einsum('bqhr,bkr->bhqk', qr, k_rope, preferred_element_type=jnp.float32)
        q_pos = start + jnp.arange(Q_BLOCK, dtype=jnp.int32)
        causal = key_pos[None, :] <= q_pos[:, None]
        s = jnp.where(causal, s * scale, neg)
        p = jax.nn.softmax(s, axis=-1).astype(v.dtype)
        return jnp.einsum('bhqk,bkhd->bqhd', p, v)

    o = lax.map(attend, (qn_blocks, qr_blocks, starts))
    o = o.transpose(1, 0, 2, 3, 4).reshape(B, S, MLA_HEADS * V_HEAD)
    return o @ w_o


def rglru_mixer(u, w_in, conv_w, conv_b, gate_a_w, gate_a_b, gate_x_w, gate_x_b, a_param, w_out):
    B, S, _ = u.shape
    y_branch, x_branch = jnp.split(u @ w_in, 2, axis=-1)
    y_branch = jax.nn.gelu(y_branch, approximate=True)
    x_branch = lax.conv_general_dilated(
        x_branch, conv_w[:, None, :], window_strides=(1,),
        padding=[(CONV_WIDTH - 1, 0)], dimension_numbers=('NWC', 'WIO', 'NWC'),
        feature_group_count=LRU_WIDTH) + conv_b
    xb = x_branch.reshape(B, S, LRU_BLOCKS, LRU_BLOCK)
    gate_r = jax.nn.sigmoid(jnp.einsum('bsni,nij->bsnj', xb, gate_a_w) + gate_a_b)
    gate_i = jax.nn.sigmoid(jnp.einsum('bsni,nij->bsnj', xb, gate_x_w) + gate_x_b)
    gate_r = gate_r.reshape(B, S, LRU_WIDTH).astype(jnp.float32)
    gate_i = gate_i.reshape(B, S, LRU_WIDTH).astype(jnp.float32)
    log_a = -LRU_C * gate_r * jax.nn.softplus(-a_param.astype(jnp.float32))
    a = jnp.exp(log_a)
    mult = jnp.sqrt(-jnp.expm1(2.0 * log_a))
    b = mult * (gate_i * x_branch.astype(jnp.float32))

    def combine(left, right):
        a_l, b_l = left
        a_r, b_r = right
        return a_l * a_r, a_r * b_l + b_r

    _, h = lax.associative_scan(combine, (a, b), axis=1)
    return (h.astype(u.dtype) * y_branch) @ w_out


def _normal(key, shape, fan_in):
    return jax.random.normal(key, shape, jnp.float32) * (fan_in ** -0.5)


def _gain(key, shape):
    return 1.0 + 0.02 * jax.random.normal(key, shape, jnp.float32)


def setup_inputs(seed: int = 0) -> dict:
    key = jax.random.key(seed)
    ks = jax.random.split(key, 30)
    x = jax.random.normal(ks[0], (BATCH, SEQ, D_MODEL), jnp.float32)
    offset = jax.random.randint(ks[1], (BATCH, 1), 0, 1024, dtype=jnp.int32)
    positions = offset + jnp.arange(SEQ, dtype=jnp.int32)[None, :]
    u = jax.random.uniform(ks[2], (N_LAYERS_B, LRU_WIDTH), jnp.float32, 0.9, 0.999)
    s = u ** (1.0 / LRU_C)
    return {
        "x": x,
        "positions": positions,
        "norm_ffn1": _gain(ks[3], (DEPTH, D_MODEL)),
        "ffn1_in": _normal(ks[4], (DEPTH, D_MODEL, 2 * D_FF), D_MODEL),
        "ffn1_out": _normal(ks[5], (DEPTH, D_FF, D_MODEL), D_FF),
        "norm_mix": _gain(ks[6], (DEPTH, D_MODEL)),
        "norm_ffn2": _gain(ks[7], (DEPTH, D_MODEL)),
        "ffn2_in": _normal(ks[8], (DEPTH, D_MODEL, 2 * D_FF), D_MODEL),
        "ffn2_out": _normal(ks[9], (DEPTH, D_FF, D_MODEL), D_FF),
        "mla_in": _normal(ks[10], (N_LAYERS_A, D_MODEL, Q_LORA + KV_LORA + QK_ROPE), D_MODEL),
        "mla_q_norm": _gain(ks[11], (N_LAYERS_A, Q_LORA)),
        "mla_kv_norm": _gain(ks[12], (N_LAYERS_A, KV_LORA)),
        "mla_w_uq": _normal(ks[13], (N_LAYERS_A, Q_LORA, MLA_HEADS * (QK_NOPE + QK_ROPE)), Q_LORA),
        "mla_w_ukv": _normal(ks[14], (N_LAYERS_A, KV_LORA, MLA_HEADS * (QK_NOPE + V_HEAD)), KV_LORA),
        "mla_w_o": _normal(ks[15], (N_LAYERS_A, MLA_HEADS * V_HEAD, D_MODEL), MLA_HEADS * V_HEAD),
        "rg_in": _normal(ks[16], (N_LAYERS_B, D_MODEL, 2 * LRU_WIDTH), D_MODEL),
        "rg_conv_w": _normal(ks[17], (N_LAYERS_B, CONV_WIDTH, LRU_WIDTH), CONV_WIDTH),
        "rg_conv_b": 0.01 * jax.random.normal(ks[18], (N_LAYERS_B, LRU_WIDTH), jnp.float32),
        "rg_gate_a_w": _normal(ks[19], (N_LAYERS_B, LRU_BLOCKS, LRU_BLOCK, LRU_BLOCK), LRU_BLOCK),
        "rg_gate_a_b": 0.01 * jax.random.normal(ks[20], (N_LAYERS_B, LRU_BLOCKS, LRU_BLOCK), jnp.float32),
        "rg_gate_x_w": _normal(ks[21], (N_LAYERS_B, LRU_BLOCKS, LRU_BLOCK, LRU_BLOCK), LRU_BLOCK),
        "rg_gate_x_b": 0.01 * jax.random.normal(ks[22], (N_LAYERS_B, LRU_BLOCKS, LRU_BLOCK), jnp.float32),
        "rg_a_param": jnp.log(s) - jnp.log1p(-s),
        "rg_out": _normal(ks[23], (N_LAYERS_B, LRU_WIDTH, D_MODEL), LRU_WIDTH),
        "norm_final": _gain(ks[24], (D_MODEL,)),
    }


def reference(x, positions, norm_ffn1, ffn1_in, ffn1_out, norm_mix, norm_ffn2, ffn2_in, ffn2_out,
              mla_in, mla_q_norm, mla_kv_norm, mla_w_uq, mla_w_ukv, mla_w_o,
              rg_in, rg_conv_w, rg_conv_b, rg_gate_a_w, rg_gate_a_b, rg_gate_x_w, rg_gate_x_b,
              rg_a_param, rg_out, norm_final):
    h = x
    for i in range(DEPTH):
        h = h + 0.5 * swiglu_ffn(rms_norm(h, norm_ffn1[i]), ffn1_in[i], ffn1_out[i])
        u = rms_norm(h, norm_mix[i])
        j = i // N_MIXERS
        if i % N_MIXERS == 0:
            m = mla_mixer(u, positions, mla_in[j], mla_q_norm[j], mla_kv_norm[j],
                          mla_w_uq[j], mla_w_ukv[j], mla_w_o[j])
        else:
            m = rglru_mixer(u, rg_in[j], rg_conv_w[j], rg_conv_b[j], rg_gate_a_w[j], rg_gate_a_b[j],
                            rg_gate_x_w[j], rg_gate_x_b[j], rg_a_param[j], rg_out[j])
        h = h + m
        h = h + 0.5 * swiglu_ffn(rms_norm(h, norm_ffn2[i]), ffn2_in[i], ffn2_out[i])
    return rms_norm(h, norm_final)
```

```python
import dataclasses
import functools
import math

import jax
import jax.numpy as jnp
from jax import lax
from jax.experimental import pallas as pl
from jax.experimental.pallas import tpu as pltpu

F32 = jnp.float32
BF16 = jnp.bfloat16

LANES = 128
SUBLANES = 8
V7X_VMEM_BYTES = 64 * 1024 * 1024
VMEM_BUDGET = V7X_VMEM_BYTES - 8 * 1024 * 1024

MLA_HEADS = 32
NORM_EPS = 1e-6
ROPE_THETA = 10000.0
LRU_C = 8.0
MASK_VALUE = -1e30


@dataclasses.dataclass(frozen=True)
class Dims:
    batch: int
    seq: int
    d_model: int
    d_ff: int
    heads: int
    qk_nope: int
    qk_rope: int
    v_head: int
    q_lora: int
    kv_lora: int
    lru_width: int
    lru_blocks: int
    conv_width: int

    @property
    def tokens(self):
        return self.batch * self.seq

    @property
    def lru_block(self):
        return self.lru_width // self.lru_blocks

    @property
    def q_head_pad(self):
        return _round_up(self.qk_nope + self.qk_rope, LANES)


def _round_up(x, m):
    return (x + m - 1) // m * m


def _pick(n, target, quantum):
    if n <= target:
        return n
    best = None
    for d in range(quantum, target + 1, quantum):
        if n % d == 0:
            best = d
    if best is None:
        raise ValueError(f"no block of multiple {quantum} <= {target} divides {n}")
    return best


def _params(semantics, vmem_bytes):
    limit = min(max(int(vmem_bytes * 1.25) + (4 << 20), 32 << 20), VMEM_BUDGET)
    return pltpu.CompilerParams(dimension_semantics=semantics, vmem_limit_bytes=limit)


def _rms(x, g):
    y = x * lax.rsqrt(jnp.mean(x * x, axis=-1, keepdims=True) + NORM_EPS)
    return y * g


def _rmsnorm_kernel(x_ref, g_ref, o_ref):
    o_ref[...] = _rms(x_ref[...], g_ref[...]).astype(o_ref.dtype)


def rmsnorm(x, g, out_dtype):
    t, d = x.shape
    bm = _pick(t, 512, SUBLANES)
    vmem = 2 * bm * d * (4 + jnp.dtype(out_dtype).itemsize)
    return pl.pallas_call(
        _rmsnorm_kernel,
        grid=(t // bm,),
        in_specs=[pl.BlockSpec((bm, d), lambda i: (i, 0)),
                  pl.BlockSpec((1, d), lambda i: (0, 0))],
        out_specs=pl.BlockSpec((bm, d), lambda i: (i, 0)),
        out_shape=jax.ShapeDtypeStruct((t, d), out_dtype),
        compiler_params=_params(("parallel",), vmem),
        name="rmsnorm",
    )(x, g.reshape(1, d))


def _swiglu_kernel(x_ref, wg_ref, wu_ref, o_ref):
    x = x_ref[...]
    g = jnp.dot(x, wg_ref[...], preferred_element_type=F32)
    u = jnp.dot(x, wu_ref[...], preferred_element_type=F32)
    o_ref[...] = (jax.nn.silu(g) * u).astype(o_ref.dtype)


def swiglu_in(x, w_in):
    t, d = x.shape
    f = w_in.shape[1] // 2
    bm = _pick(t, 1024, SUBLANES)
    bn = _pick(f, 512, LANES)
    nj = f // bn
    vmem = 2 * (bm * d * 2 + 2 * d * bn * 2 + bm * bn * 2) + 3 * bm * bn * 4
    return pl.pallas_call(
        _swiglu_kernel,
        grid=(t // bm, nj),
        in_specs=[pl.BlockSpec((bm, d), lambda i, j: (i, 0)),
                  pl.BlockSpec((d, bn), lambda i, j: (0, j)),
                  pl.BlockSpec((d, bn), lambda i, j: (0, j + nj))],
        out_specs=pl.BlockSpec((bm, bn), lambda i, j: (i, j)),
        out_shape=jax.ShapeDtypeStruct((t, f), BF16),
        compiler_params=_params(("parallel", "parallel"), vmem),
        name="swiglu_in",
    )(x, w_in, w_in)


def _mm_res_kernel(a_ref, w_ref, r_ref, o_ref, *scratch, scale, nk):
    def finish(acc):
        if scale == 1.0:
            o_ref[...] = r_ref[...] + acc
        else:
            o_ref[...] = r_ref[...] + scale * acc

    part = jnp.dot(a_ref[...], w_ref[...], preferred_element_type=F32)
    if nk == 1:
        finish(part)
        return
    (acc_ref,) = scratch
    k = pl.program_id(2)

    @pl.when(k == 0)
    def _():
        acc_ref[...] = part

    @pl.when(k > 0)
    def _():
        acc_ref[...] += part

    @pl.when(k == nk - 1)
    def _():
        finish(acc_ref[...])


def matmul_residual(a, w, res, scale):
    t, kdim = a.shape
    n = w.shape[1]
    bm = _pick(t, 1024, SUBLANES)
    bn = _pick(n, 1024, LANES)
    bk = _pick(kdim, 2048, LANES)
    nk = kdim // bk
    vmem = 2 * (bm * bk * 2 + bk * bn * 2 + 2 * bm * bn * 4) + 2 * bm * bn * 4
    scratch = [pltpu.VMEM((bm, bn), F32)] if nk > 1 else []
    return pl.pallas_call(
        functools.partial(_mm_res_kernel, scale=scale, nk=nk),
        grid=(t // bm, n // bn, nk),
        in_specs=[pl.BlockSpec((bm, bk), lambda i, j, k: (i, k)),
                  pl.BlockSpec((bk, bn), lambda i, j, k: (k, j)),
                  pl.BlockSpec((bm, bn), lambda i, j, k: (i, j))],
        out_specs=pl.BlockSpec((bm, bn), lambda i, j, k: (i, j)),
        out_shape=jax.ShapeDtypeStruct((t, n), F32),
        scratch_shapes=scratch,
        compiler_params=_params(("parallel", "parallel", "arbitrary"), vmem),
        name="matmul_residual",
    )(a, w, res)


def _rope_table_kernel(pos_ref, freq_ref, cos_ref, sina_ref, sinb_ref, *, half):
    ang = pos_ref[...].astype(F32) * freq_ref[...]
    lane = lax.broadcasted_iota(jnp.int32, ang.shape, 1)
    c = jnp.cos(ang)
    s = jnp.sin(ang)
    cos_ref[...] = jnp.where(lane < 2 * half, c, 0.0)
    sina_ref[...] = jnp.where(lane < half, -s, 0.0)
    sinb_ref[...] = jnp.where((lane >= half) & (lane < 2 * half), s, 0.0)


def rope_tables(positions, rope_dim):
    half = rope_dim // 2
    t = positions.size
    inv_freq = ROPE_THETA ** (-jnp.arange(half, dtype=F32) / half)
    freq_row = jnp.zeros((1, LANES), F32).at[0, :half].set(inv_freq).at[0, half:2 * half].set(inv_freq)
    bm = _pick(t, 1024, SUBLANES)
    out = jax.ShapeDtypeStruct((t, LANES), F32)
    spec = pl.BlockSpec((bm, LANES), lambda i: (i, 0))
    return pl.pallas_call(
        functools.partial(_rope_table_kernel, half=half),
        grid=(t // bm,),
        in_specs=[pl.BlockSpec((bm, 1), lambda i: (i, 0)),
                  pl.BlockSpec((1, LANES), lambda i: (0, 0))],
        out_specs=[spec, spec, spec],
        out_shape=[out, out, out],
        compiler_params=_params(("parallel",), 8 * bm * LANES * 4),
        name="rope_tables",
    )(positions.reshape(t, 1), freq_row)


def _rope_slot(x, cos, sina, sinb, half):
    x2_on_x1 = pltpu.roll(x, LANES - half, axis=1)
    x1_on_x2 = pltpu.roll(x, half, axis=1)
    return x * cos + x2_on_x1 * sina + x1_on_x2 * sinb


def _mla_in_kernel(u_ref, wq_ref, wkv_ref, wkr_ref, gq_ref, gkv_ref, cos_ref, sina_ref, sinb_ref,
                   cq_ref, ckv_ref, kr_ref, *, half):
    u = u_ref[...]
    cq = jnp.dot(u, wq_ref[...], preferred_element_type=F32)
    cq_ref[...] = _rms(cq, gq_ref[...]).astype(cq_ref.dtype)
    ckv = jnp.dot(u, wkv_ref[...], preferred_element_type=F32)
    ckv_ref[...] = _rms(ckv, gkv_ref[...]).astype(ckv_ref.dtype)
    kr = jnp.dot(u, wkr_ref[...], preferred_element_type=F32)
    kr_ref[...] = _rope_slot(kr, cos_ref[...], sina_ref[...], sinb_ref[...], half).astype(kr_ref.dtype)


def mla_in_proj(u, wq, wkv, wkr, gq, gkv, tables, half):
    t, d = u.shape
    nq, nkv = wq.shape[1], wkv.shape[1]
    bm = _pick(t, 512, SUBLANES)
    row = lambda i: (i, 0)
    fixed = lambda i: (0, 0)
    vmem = 2 * (bm * d * 2 + d * (nq + nkv + LANES) * 2 + bm * (nq + nkv + LANES) * 2
                + 3 * bm * LANES * 4) + 2 * bm * (nq + nkv) * 4
    return pl.pallas_call(
        functools.partial(_mla_in_kernel, half=half),
        grid=(t // bm,),
        in_specs=[pl.BlockSpec((bm, d), row),
                  pl.BlockSpec((d, nq), fixed),
                  pl.BlockSpec((d, nkv), fixed),
                  pl.BlockSpec((d, LANES), fixed),
                  pl.BlockSpec((1, nq), fixed),
                  pl.BlockSpec((1, nkv), fixed),
                  pl.BlockSpec((bm, LANES), row),
                  pl.BlockSpec((bm, LANES), row),
                  pl.BlockSpec((bm, LANES), row)],
        out_specs=[pl.BlockSpec((bm, nq), row),
                   pl.BlockSpec((bm, nkv), row),
                   pl.BlockSpec((bm, LANES), row)],
        out_shape=[jax.ShapeDtypeStruct((t, nq), BF16),
                   jax.ShapeDtypeStruct((t, nkv), BF16),
                   jax.ShapeDtypeStruct((t, LANES), BF16)],
        compiler_params=_params(("parallel",), vmem),
        name="mla_in_proj",
    )(u, wq, wkv, wkr, gq.reshape(1, nq), gkv.reshape(1, nkv), *tables)


def _q_proj_kernel(c_ref, w_ref, cos_ref, sina_ref, sinb_ref, o_ref, *, half, nope, slot, scale):
    q = jnp.dot(c_ref[...], w_ref[...], preferred_element_type=F32)
    cos, sina, sinb = cos_ref[...], sina_ref[...], sinb_ref[...]
    for h in range(q.shape[1] // slot):
        base = h * slot
        o_ref[:, base:base + nope] = (q[:, base:base + nope] * scale).astype(o_ref.dtype)
        r = _rope_slot(q[:, base + nope:base + slot], cos, sina, sinb, half)
        o_ref[:, base + nope:base + slot] = (r * scale).astype(o_ref.dtype)


def q_proj(c, w, tables, dims, scale):
    t, kdim = c.shape
    n = w.shape[1]
    slot = dims.q_head_pad
    bm = _pick(t, 1024, SUBLANES)
    bn = _pick(n, 4 * slot, slot)
    row = lambda i, j: (i, 0)
    vmem = 2 * (bm * kdim * 2 + kdim * bn * 2 + bm * bn * 2 + 3 * bm * LANES * 4) + 2 * bm * bn * 4
    return pl.pallas_call(
        functools.partial(_q_proj_kernel, half=dims.qk_rope // 2, nope=dims.qk_nope, slot=slot,
                          scale=scale),
        grid=(t // bm, n // bn),
        in_specs=[pl.BlockSpec((bm, kdim), row),
                  pl.BlockSpec((kdim, bn), lambda i, j: (0, j)),
                  pl.BlockSpec((bm, LANES), row),
                  pl.BlockSpec((bm, LANES), row),
                  pl.BlockSpec((bm, LANES), row)],
        out_specs=pl.BlockSpec((bm, bn), lambda i, j: (i, j)),
        out_shape=jax.ShapeDtypeStruct((t, n), BF16),
        compiler_params=_params(("parallel", "parallel"), vmem),
        name="q_proj",
    )(c, w, *tables)


def _kv_proj_kernel(c_ref, wk_ref, wv_ref, kr_ref, k_ref, v_ref, *, nope, slot):
    c = c_ref[...]
    kn = jnp.dot(c, wk_ref[...], preferred_element_type=F32)
    v_ref[...] = jnp.dot(c, wv_ref[...], preferred_element_type=F32).astype(v_ref.dtype)
    kr = kr_ref[...]
    for h in range(kn.shape[1] // nope):
        k_ref[:, h * slot:h * slot + nope] = kn[:, h * nope:(h + 1) * nope].astype(k_ref.dtype)
        k_ref[:, h * slot + nope:(h + 1) * slot] = kr


def kv_proj(c, wk, wv, kr, dims):
    t, kdim = c.shape
    nope, vh, slot = dims.qk_nope, dims.v_head, dims.q_head_pad
    hb = _pick(dims.heads, 8, 1)
    bm = _pick(t, 1024, SUBLANES)
    row = lambda i, j: (i, 0)
    col = lambda i, j: (0, j)
    out = lambda i, j: (i, j)
    vmem = 2 * (bm * kdim * 2 + kdim * hb * (nope + vh) * 2 + bm * LANES * 2
                + bm * hb * (slot + vh) * 2) + bm * hb * (nope + vh) * 4
    return pl.pallas_call(
        functools.partial(_kv_proj_kernel, nope=nope, slot=slot),
        grid=(t // bm, dims.heads // hb),
        in_specs=[pl.BlockSpec((bm, kdim), row),
                  pl.BlockSpec((kdim, hb * nope), col),
                  pl.BlockSpec((kdim, hb * vh), col),
                  pl.BlockSpec((bm, slot - nope), row)],
        out_specs=[pl.BlockSpec((bm, hb * slot), out),
                   pl.BlockSpec((bm, hb * vh), out)],
        out_shape=[jax.ShapeDtypeStruct((t, dims.heads * slot), BF16),
                   jax.ShapeDtypeStruct((t, dims.heads * vh), BF16)],
        compiler_params=_params(("parallel", "parallel"), vmem),
        name="kv_proj",
    )(c, wk, wv, kr)


def _attn_kernel(q_ref, k_ref, v_ref, o_ref, m_ref, l_ref, acc_ref, *, blk):
    qi = pl.program_id(2)
    q = q_ref[...]
    m_ref[...] = jnp.full(m_ref.shape, -jnp.inf, F32)
    l_ref[...] = jnp.zeros(l_ref.shape, F32)
    acc_ref[...] = jnp.zeros(acc_ref.shape, F32)

    def step(kb, masked):
        start = pl.multiple_of(kb * blk, blk)
        k = k_ref[pl.ds(start, blk), :]
        v = v_ref[pl.ds(start, blk), :]
        s = lax.dot_general(q, k, (((1,), (1,)), ((), ())), preferred_element_type=F32)
        if masked:
            row = lax.broadcasted_iota(jnp.int32, s.shape, 0)
            col = lax.broadcasted_iota(jnp.int32, s.shape, 1)
            s = jnp.where(col <= row, s, MASK_VALUE)
        m_old = m_ref[...]
        m_new = jnp.maximum(m_old, jnp.max(s, axis=-1, keepdims=True))
        alpha = jnp.exp(m_old - m_new)
        p = jnp.exp(s - m_new)
        l_ref[...] = alpha * l_ref[...] + jnp.sum(p, axis=-1, keepdims=True)
        acc_ref[...] = alpha * acc_ref[...] + jnp.dot(p.astype(v.dtype), v, preferred_element_type=F32)
        m_ref[...] = m_new

    def body(kb, carry):
        step(kb, masked=False)
        return carry

    lax.fori_loop(0, qi, body, 0)
    step(qi, masked=True)
    o_ref[...] = (acc_ref[...] / l_ref[...]).astype(o_ref.dtype)


def attention(q, k, v, dims):
    b, s, h = dims.batch, dims.seq, dims.heads
    slot, vh = dims.q_head_pad, dims.v_head
    blk = _pick(s, 512, LANES)
    nq = s // blk
    vmem = 2 * (blk * slot * 2 + s * slot * 2 + s * vh * 2 + blk * vh * 2) + 6 * blk * blk * 4
    return pl.pallas_call(
        functools.partial(_attn_kernel, blk=blk),
        grid=(b, h, nq),
        in_specs=[pl.BlockSpec((blk, slot), lambda bi, hi, qi: (bi * nq + qi, hi)),
                  pl.BlockSpec((s, slot), lambda bi, hi, qi: (bi, hi)),
                  pl.BlockSpec((s, vh), lambda bi, hi, qi: (bi, hi))],
        out_specs=pl.BlockSpec((blk, vh), lambda bi, hi, qi: (bi * nq + qi, hi)),
        out_shape=jax.ShapeDtypeStruct((b * s, h * vh), BF16),
        scratch_shapes=[pltpu.VMEM((blk, 1), F32), pltpu.VMEM((blk, 1), F32),
                        pltpu.VMEM((blk, vh), F32)],
        compiler_params=_params(("parallel", "parallel", "parallel"), vmem),
        name="attention",
    )(q, k, v)


def _rg_in_kernel(u_ref, wy_ref, wx_ref, y_ref, x_ref):
    u = u_ref[...]
    y = jnp.dot(u, wy_ref[...], preferred_element_type=F32)
    y_ref[...] = jax.nn.gelu(y, approximate=True).astype(y_ref.dtype)
    x_ref[...] = jnp.dot(u, wx_ref[...], preferred_element_type=F32)


def rg_in_proj(u, w_in):
    t, d = u.shape
    w = w_in.shape[1] // 2
    bm = _pick(t, 1024, SUBLANES)
    bn = _pick(w, 512, LANES)
    nj = w // bn
    vmem = 2 * (bm * d * 2 + 2 * d * bn * 2 + bm * bn * 6) + 3 * bm * bn * 4
    return pl.pallas_call(
        _rg_in_kernel,
        grid=(t // bm, nj),
        in_specs=[pl.BlockSpec((bm, d), lambda i, j: (i, 0)),
                  pl.BlockSpec((d, bn), lambda i, j: (0, j)),
                  pl.BlockSpec((d, bn), lambda i, j: (0, j + nj))],
        out_specs=[pl.BlockSpec((bm, bn), lambda i, j: (i, j)),
                   pl.BlockSpec((bm, bn), lambda i, j: (i, j))],
        out_shape=[jax.ShapeDtypeStruct((t, w), BF16), jax.ShapeDtypeStruct((t, w), F32)],
        compiler_params=_params(("parallel", "parallel"), vmem),
        name="rg_in_proj",
    )(u, w_in, w_in)


def _rglru_kernel(x_ref, y_ref, cw_ref, cb_ref, wa_ref, ba_ref, wx_ref, bx_ref, ap_ref, o_ref,
                  xe_ref, a_ref, b_ref, h_ref, *, tc, lru_block, conv_width):
    @pl.when(pl.program_id(2) == 0)
    def _():
        xe_ref[0:SUBLANES, :] = jnp.zeros((SUBLANES, xe_ref.shape[1]), F32)
        h_ref[...] = jnp.zeros(h_ref.shape, F32)

    x = x_ref[...]
    xe_ref[SUBLANES:, :] = x
    xc = x * cw_ref[conv_width - 1:conv_width, :] + cb_ref[...]
    for d in range(1, conv_width):
        xc = xc + xe_ref[SUBLANES - d:SUBLANES - d + tc, :] * cw_ref[conv_width - 1 - d:conv_width - d, :]
    xe_ref[0:SUBLANES, :] = x[tc - SUBLANES:, :]

    neg_c_softplus = -LRU_C * jax.nn.softplus(-ap_ref[...])
    for n in range(x.shape[1] // lru_block):
        sl = slice(n * lru_block, (n + 1) * lru_block)
        xb = xc[:, sl]
        xb16 = xb.astype(BF16)
        gate_r = jax.nn.sigmoid(jnp.dot(xb16, wa_ref[n], preferred_element_type=F32) + ba_ref[:, sl])
        gate_i = jax.nn.sigmoid(jnp.dot(xb16, wx_ref[n], preferred_element_type=F32) + bx_ref[:, sl])
        log_a = gate_r * neg_c_softplus[:, sl]
        a = jnp.exp(log_a)
        a_ref[:, sl] = a
        b_ref[:, sl] = jnp.sqrt(-jnp.tanh(log_a) * (a * a + 1.0)) * (gate_i * xb)

    row = lax.broadcasted_iota(jnp.int32, (SUBLANES, x.shape[1]), 0)

    def group(g, h_prev):
        start = pl.multiple_of(g * SUBLANES, SUBLANES)
        a = a_ref[pl.ds(start, SUBLANES), :]
        b = b_ref[pl.ds(start, SUBLANES), :]
        d = 1
        while d < SUBLANES:
            keep = row >= d
            b = jnp.where(keep, a * pltpu.roll(b, d, axis=0) + b, b)
            a = jnp.where(keep, a * pltpu.roll(a, d, axis=0), a)
            d *= 2
        h = a * h_prev + b
        b_ref[pl.ds(start, SUBLANES), :] = h
        return jnp.broadcast_to(h[SUBLANES - 1:SUBLANES, :], h.shape)

    h_ref[...] = lax.fori_loop(0, tc // SUBLANES, group, h_ref[...], unroll=4)
    o_ref[...] = (b_ref[...] * y_ref[...].astype(F32)).astype(o_ref.dtype)


def rglru_core(x, y, conv_w, conv_b, wa, ba, wx, bx, a_param, dims):
    b, s, w = dims.batch, dims.seq, dims.lru_width
    lb = dims.lru_block
    cw = _pick(w, 2 * lb, lb)
    tc = _pick(s, 512, SUBLANES)
    nt = s // tc
    npb = cw // lb
    tile = lambda bi, ci, ti: (bi * nt + ti, ci)
    chan = lambda bi, ci, ti: (0, ci)
    wblk = lambda bi, ci, ti: (ci, 0, 0)
    vmem = 2 * (tc * cw * (4 + 2 + 2) + 2 * npb * lb * lb * 2) + 8 * tc * cw * 4
    return pl.pallas_call(
        functools.partial(_rglru_kernel, tc=tc, lru_block=lb, conv_width=dims.conv_width),
        grid=(b, w // cw, nt),
        in_specs=[pl.BlockSpec((tc, cw), tile),
                  pl.BlockSpec((tc, cw), tile),
                  pl.BlockSpec((dims.conv_width, cw), chan),
                  pl.BlockSpec((1, cw), chan),
                  pl.BlockSpec((npb, lb, lb), wblk),
                  pl.BlockSpec((1, cw), chan),
                  pl.BlockSpec((npb, lb, lb), wblk),
                  pl.BlockSpec((1, cw), chan),
                  pl.BlockSpec((1, cw), chan)],
        out_specs=pl.BlockSpec((tc, cw), tile),
        out_shape=jax.ShapeDtypeStruct((b * s, w), BF16),
        scratch_shapes=[pltpu.VMEM((tc + SUBLANES, cw), F32),
                        pltpu.VMEM((tc, cw), F32),
                        pltpu.VMEM((tc, cw), F32),
                        pltpu.VMEM((SUBLANES, cw), F32)],
        compiler_params=_params(("parallel", "parallel", "arbitrary"), vmem),
        name="rglru_core",
    )(x, y, conv_w, conv_b.reshape(1, w), wa, ba.reshape(1, w), wx, bx.reshape(1, w),
      a_param.reshape(1, w))


def _ffn(h, g, w_in, w_out):
    act = swiglu_in(rmsnorm(h, g, BF16), w_in.astype(BF16))
    return matmul_residual(act, w_out.astype(BF16), h, 0.5)


def _pad_heads(w, dims, width):
    k = w.shape[0]
    w = w.reshape(k, dims.heads, width)
    w = jnp.pad(w, ((0, 0), (0, 0), (0, dims.q_head_pad - width)))
    return w.reshape(k, dims.heads * dims.q_head_pad)


def _mla(h, g, tables, w_in, q_norm, kv_norm, w_uq, w_ukv, w_o, dims):
    u = rmsnorm(h, g, BF16)
    ql, kvl, rope = dims.q_lora, dims.kv_lora, dims.qk_rope
    slot = dims.q_head_pad
    w_in = w_in.astype(BF16)
    w_kr = jnp.pad(w_in[:, ql + kvl:], ((0, 0), (0, slot - dims.qk_nope - rope)))
    cq, ckv, kr = mla_in_proj(u, w_in[:, :ql], w_in[:, ql:ql + kvl], w_kr, q_norm, kv_norm, tables,
                              rope // 2)
    scale = (dims.qk_nope + rope) ** -0.5
    q = q_proj(cq, _pad_heads(w_uq.astype(BF16), dims, dims.qk_nope + rope), tables, dims, scale)
    w_ukv = w_ukv.astype(BF16).reshape(kvl, dims.heads, dims.qk_nope + dims.v_head)
    w_k = w_ukv[:, :, :dims.qk_nope].reshape(kvl, dims.heads * dims.qk_nope)
    w_v = w_ukv[:, :, dims.qk_nope:].reshape(kvl, dims.heads * dims.v_head)
    k, v = kv_proj(ckv, w_k, w_v, kr, dims)
    o = attention(q, k, v, dims)
    return matmul_residual(o, w_o.astype(BF16), h, 1.0)


def _rglru(h, g, w_in, conv_w, conv_b, wa, ba, wx, bx, a_param, w_out, dims):
    u = rmsnorm(h, g, BF16)
    y, x = rg_in_proj(u, w_in.astype(BF16))
    hy = rglru_core(x, y, conv_w, conv_b, wa.astype(BF16), ba, wx.astype(BF16), bx, a_param, dims)
    return matmul_residual(hy, w_out.astype(BF16), h, 1.0)


def _forward(dims, x, positions, norm_ffn1, ffn1_in, ffn1_out, norm_mix, norm_ffn2, ffn2_in, ffn2_out,
             mla_in, mla_q_norm, mla_kv_norm, mla_w_uq, mla_w_ukv, mla_w_o,
             rg_in, rg_conv_w, rg_conv_b, rg_gate_a_w, rg_gate_a_b, rg_gate_x_w, rg_gate_x_b,
             rg_a_param, rg_out, norm_final):
    depth = norm_ffn1.shape[0]
    h = x.reshape(dims.tokens, dims.d_model)
    tables = rope_tables(positions, dims.qk_rope)
    for i in range(depth):
        h = _ffn(h, norm_ffn1[i], ffn1_in[i], ffn1_out[i])
        j = i // 2
        if i % 2 == 0:
            h = _mla(h, norm_mix[i], tables, mla_in[j], mla_q_norm[j], mla_kv_norm[j], mla_w_uq[j],
                     mla_w_ukv[j], mla_w_o[j], dims)
        else:
            h = _rglru(h, norm_mix[i], rg_in[j], rg_conv_w[j], rg_conv_b[j], rg_gate_a_w[j],
                       rg_gate_a_b[j].reshape(-1), rg_gate_x_w[j], rg_gate_x_b[j].reshape(-1),
                       rg_a_param[j], rg_out[j], dims)
        h = _ffn(h, norm_ffn2[i], ffn2_in[i], ffn2_out[i])
    out = rmsnorm(h, norm_final, x.dtype)
    return out.reshape(x.shape)


def kernel(x, positions, norm_ffn1, ffn1_in, ffn1_out, norm_mix, norm_ffn2, ffn2_in, ffn2_out, mla_in, mla_q_norm, mla_kv_norm, mla_w_uq, mla_w_ukv, mla_w_o, rg_in, rg_conv_w, rg_conv_b, rg_gate_a_w, rg_gate_a_b, rg_gate_x_w, rg_gate_x_b, rg_a_param, rg_out, norm_final):
    batch, seq, d_model = x.shape
    heads = MLA_HEADS
    v_head = mla_w_o.shape[1] // heads
    qk_nope = mla_w_ukv.shape[2] // heads - v_head
    dims = Dims(
        batch=batch, seq=seq, d_model=d_model, d_ff=ffn1_out.shape[1], heads=heads,
        qk_nope=qk_nope, qk_rope=mla_w_uq.shape[2] // heads - qk_nope, v_head=v_head,
        q_lora=mla_q_norm.shape[1], kv_lora=mla_kv_norm.shape[1],
        lru_width=rg_a_param.shape[1], lru_blocks=rg_gate_a_w.shape[1],
        conv_width=rg_conv_w.shape[1])
    return _forward(dims, x, positions, norm_ffn1, ffn1_in, ffn1_out, norm_mix, norm_ffn2, ffn2_in,
                    ffn2_out, mla_in, mla_q_norm, mla_kv_norm, mla_w_uq, mla_w_ukv, mla_w_o,
                    rg_in, rg_conv_w, rg_conv_b, rg_gate_a_w, rg_gate_a_b, rg_gate_x_w, rg_gate_x_b,
                    rg_a_param, rg_out, norm_final)
```

```python
import dataclasses
import functools
import math

import jax
import jax.numpy as jnp
from jax import lax
from jax.experimental import pallas as pl
from jax.experimental.pallas import tpu as pltpu

F32 = jnp.float32
BF16 = jnp.bfloat16

LANES = 128
SUBLANES = 8
V7X_VMEM_BYTES = 64 * 1024 * 1024
VMEM_BUDGET = V7X_VMEM_BYTES - 8 * 1024 * 1024

MLA_HEADS = 32
NORM_EPS = 1e-6
ROPE_THETA = 10000.0
LRU_C = 8.0
MASK_VALUE = -1e30


@dataclasses.dataclass(frozen=True)
class Dims:
    batch: int
    seq: int
    d_model: int
    d_ff: int
    heads: int
    qk_nope: int
    qk_rope: int
    v_head: int
    q_lora: int
    kv_lora: int
    lru_width: int
    lru_blocks: int
    conv_width: int

    @property
    def tokens(self):
        return self.batch * self.seq

    @property
    def lru_block(self):
        return self.lru_width // self.lru_blocks

    @property
    def q_head_pad(self):
        return _round_up(self.qk_nope + self.qk_rope, LANES)


def _round_up(x, m):
    return (x + m - 1) // m * m


def _pick(n, target, quantum):
    if n <= target:
        return n
    best = None
    for d in range(quantum, target + 1, quantum):
        if n % d == 0:
            best = d
    if best is None:
        raise ValueError(f"no block of multiple {quantum} <= {target} divides {n}")
    return best


def _params(semantics, vmem_bytes):
    limit = min(max(int(vmem_bytes * 1.25) + (4 << 20), 32 << 20), VMEM_BUDGET)
    return pltpu.CompilerParams(dimension_semantics=semantics, vmem_limit_bytes=limit)


def _rms(x, g):
    y = x * lax.rsqrt(jnp.mean(x * x, axis=-1, keepdims=True) + NORM_EPS)
    return y * g


def _rmsnorm_kernel(x_ref, g_ref, o_ref):
    o_ref[...] = _rms(x_ref[...], g_ref[...]).astype(o_ref.dtype)


def rmsnorm(x, g, out_dtype):
    t, d = x.shape
    bm = _pick(t, 512, SUBLANES)
    vmem = 2 * bm * d * (4 + jnp.dtype(out_dtype).itemsize)
    return pl.pallas_call(
        _rmsnorm_kernel,
        grid=(t // bm,),
        in_specs=[pl.BlockSpec((bm, d), lambda i: (i, 0)),
                  pl.BlockSpec((1, d), lambda i: (0, 0))],
        out_specs=pl.BlockSpec((bm, d), lambda i: (i, 0)),
        out_shape=jax.ShapeDtypeStruct((t, d), out_dtype),
        compiler_params=_params(("parallel",), vmem),
        name="rmsnorm",
    )(x, g.reshape(1, d))


def _swiglu_kernel(x_ref, wg_ref, wu_ref, o_ref):
    x = x_ref[...]
    g = jnp.dot(x, wg_ref[...], preferred_element_type=F32)
    u = jnp.dot(x, wu_ref[...], preferred_element_type=F32)
    o_ref[...] = (jax.nn.silu(g) * u).astype(o_ref.dtype)


def swiglu_in(x, w_in):
    t, d = x.shape
    f = w_in.shape[1] // 2
    bm = _pick(t, 1024, SUBLANES)
    bn = _pick(f, 512, LANES)
    nj = f // bn
    vmem = 2 * (bm * d * 2 + 2 * d * bn * 2 + bm * bn * 2) + 3 * bm * bn * 4
    return pl.pallas_call(
        _swiglu_kernel,
        grid=(t // bm, nj),
        in_specs=[pl.BlockSpec((bm, d), lambda i, j: (i, 0)),
                  pl.BlockSpec((d, bn), lambda i, j: (0, j)),
                  pl.BlockSpec((d, bn), lambda i, j: (0, j + nj))],
        out_specs=pl.BlockSpec((bm, bn), lambda i, j: (i, j)),
        out_shape=jax.ShapeDtypeStruct((t, f), BF16),
        compiler_params=_params(("parallel", "parallel"), vmem),
        name="swiglu_in",
    )(x, w_in, w_in)


def _mm_res_kernel(a_ref, w_ref, r_ref, o_ref, *, scale):
    @pl.when(pl.program_id(2) == 0)
    def _():
        o_ref[...] = r_ref[...]

    part = jnp.dot(a_ref[...], w_ref[...], preferred_element_type=F32)
    if scale != 1.0:
        part = scale * part
    o_ref[...] += part


def matmul_residual(a, w, res, scale):
    t, kdim = a.shape
    n = w.shape[1]
    bm = _pick(t, 1024, SUBLANES)
    bn = _pick(n, 2048, LANES)
    bk = _pick(kdim, 1024, LANES)
    vmem = 2 * (bm * bk * 2 + bk * bn * 2 + 2 * bm * bn * 4) + bm * bn * 4
    return pl.pallas_call(
        functools.partial(_mm_res_kernel, scale=scale),
        grid=(t // bm, n // bn, kdim // bk),
        in_specs=[pl.BlockSpec((bm, bk), lambda i, j, k: (i, k)),
                  pl.BlockSpec((bk, bn), lambda i, j, k: (k, j)),
                  pl.BlockSpec((bm, bn), lambda i, j, k: (i, j))],
        out_specs=pl.BlockSpec((bm, bn), lambda i, j, k: (i, j)),
        out_shape=jax.ShapeDtypeStruct((t, n), F32),
        compiler_params=_params(("parallel", "parallel", "arbitrary"), vmem),
        name="matmul_residual",
    )(a, w, res)


def _rope_table_kernel(pos_ref, freq_ref, cos_ref, sina_ref, sinb_ref, *, half):
    ang = pos_ref[...].astype(F32) * freq_ref[...]
    lane = lax.broadcasted_iota(jnp.int32, ang.shape, 1)
    c = jnp.cos(ang)
    s = jnp.sin(ang)
    cos_ref[...] = jnp.where(lane < 2 * half, c, 0.0)
    sina_ref[...] = jnp.where(lane < half, -s, 0.0)
    sinb_ref[...] = jnp.where((lane >= half) & (lane < 2 * half), s, 0.0)


def rope_tables(positions, rope_dim):
    half = rope_dim // 2
    t = positions.size
    inv_freq = ROPE_THETA ** (-jnp.arange(half, dtype=F32) / half)
    freq_row = jnp.zeros((1, LANES), F32).at[0, :half].set(inv_freq).at[0, half:2 * half].set(inv_freq)
    bm = _pick(t, 1024, SUBLANES)
    out = jax.ShapeDtypeStruct((t, LANES), F32)
    spec = pl.BlockSpec((bm, LANES), lambda i: (i, 0))
    return pl.pallas_call(
        functools.partial(_rope_table_kernel, half=half),
        grid=(t // bm,),
        in_specs=[pl.BlockSpec((bm, 1), lambda i: (i, 0)),
                  pl.BlockSpec((1, LANES), lambda i: (0, 0))],
        out_specs=[spec, spec, spec],
        out_shape=[out, out, out],
        compiler_params=_params(("parallel",), 8 * bm * LANES * 4),
        name="rope_tables",
    )(positions.reshape(t, 1), freq_row)


def _rope_slot(x, cos, sina, sinb, half):
    x2_on_x1 = pltpu.roll(x, LANES - half, axis=1)
    x1_on_x2 = pltpu.roll(x, half, axis=1)
    return x * cos + x2_on_x1 * sina + x1_on_x2 * sinb


def _mla_in_kernel(u_ref, wq_ref, wkv_ref, wkr_ref, gq_ref, gkv_ref, cos_ref, sina_ref, sinb_ref,
                   cq_ref, ckv_ref, kr_ref, *, half):
    u = u_ref[...]
    cq = jnp.dot(u, wq_ref[...], preferred_element_type=F32)
    cq_ref[...] = _rms(cq, gq_ref[...]).astype(cq_ref.dtype)
    ckv = jnp.dot(u, wkv_ref[...], preferred_element_type=F32)
    ckv_ref[...] = _rms(ckv, gkv_ref[...]).astype(ckv_ref.dtype)
    kr = jnp.dot(u, wkr_ref[...], preferred_element_type=F32)
    kr_ref[...] = _rope_slot(kr, cos_ref[...], sina_ref[...], sinb_ref[...], half).astype(kr_ref.dtype)


def mla_in_proj(u, wq, wkv, wkr, gq, gkv, tables, half):
    t, d = u.shape
    nq, nkv = wq.shape[1], wkv.shape[1]
    bm = _pick(t, 512, SUBLANES)
    row = lambda i: (i, 0)
    fixed = lambda i: (0, 0)
    vmem = 2 * (bm * d * 2 + d * (nq + nkv + LANES) * 2 + bm * (nq + nkv + LANES) * 2
                + 3 * bm * LANES * 4) + 2 * bm * (nq + nkv) * 4
    return pl.pallas_call(
        functools.partial(_mla_in_kernel, half=half),
        grid=(t // bm,),
        in_specs=[pl.BlockSpec((bm, d), row),
                  pl.BlockSpec((d, nq), fixed),
                  pl.BlockSpec((d, nkv), fixed),
                  pl.BlockSpec((d, LANES), fixed),
                  pl.BlockSpec((1, nq), fixed),
                  pl.BlockSpec((1, nkv), fixed),
                  pl.BlockSpec((bm, LANES), row),
                  pl.BlockSpec((bm, LANES), row),
                  pl.BlockSpec((bm, LANES), row)],
        out_specs=[pl.BlockSpec((bm, nq), row),
                   pl.BlockSpec((bm, nkv), row),
                   pl.BlockSpec((bm, LANES), row)],
        out_shape=[jax.ShapeDtypeStruct((t, nq), BF16),
                   jax.ShapeDtypeStruct((t, nkv), BF16),
                   jax.ShapeDtypeStruct((t, LANES), BF16)],
        compiler_params=_params(("parallel",), vmem),
        name="mla_in_proj",
    )(u, wq, wkv, wkr, gq.reshape(1, nq), gkv.reshape(1, nkv), *tables)


def _q_proj_kernel(c_ref, w_ref, cos_ref, sina_ref, sinb_ref, o_ref, *, half, nope, slot, scale):
    q = jnp.dot(c_ref[...], w_ref[...], preferred_element_type=F32)
    cos, sina, sinb = cos_ref[...], sina_ref[...], sinb_ref[...]
    for h in range(q.shape[1] // slot):
        base = h * slot
        o_ref[:, base:base + nope] = (q[:, base:base + nope] * scale).astype(o_ref.dtype)
        r = _rope_slot(q[:, base + nope:base + slot], cos, sina, sinb, half)
        o_ref[:, base + nope:base + slot] = (r * scale).astype(o_ref.dtype)


def q_proj(c, w, tables, dims, scale):
    t, kdim = c.shape
    n = w.shape[1]
    slot = dims.q_head_pad
    bm = _pick(t, 1024, SUBLANES)
    bn = _pick(n, 4 * slot, slot)
    row = lambda i, j: (i, 0)
    vmem = 2 * (bm * kdim * 2 + kdim * bn * 2 + bm * bn * 2 + 3 * bm * LANES * 4) + 2 * bm * bn * 4
    return pl.pallas_call(
        functools.partial(_q_proj_kernel, half=dims.qk_rope // 2, nope=dims.qk_nope, slot=slot,
                          scale=scale),
        grid=(t // bm, n // bn),
        in_specs=[pl.BlockSpec((bm, kdim), row),
                  pl.BlockSpec((kdim, bn), lambda i, j: (0, j)),
                  pl.BlockSpec((bm, LANES), row),
                  pl.BlockSpec((bm, LANES), row),
                  pl.BlockSpec((bm, LANES), row)],
        out_specs=pl.BlockSpec((bm, bn), lambda i, j: (i, j)),
        out_shape=jax.ShapeDtypeStruct((t, n), BF16),
        compiler_params=_params(("parallel", "parallel"), vmem),
        name="q_proj",
    )(c, w, *tables)


def _kv_proj_kernel(c_ref, wk_ref, wv_ref, kr_ref, k_ref, v_ref, *, nope, slot):
    c = c_ref[...]
    kn = jnp.dot(c, wk_ref[...], preferred_element_type=F32)
    v_ref[...] = jnp.dot(c, wv_ref[...], preferred_element_type=F32).astype(v_ref.dtype)
    kr = kr_ref[...]
    for h in range(kn.shape[1] // nope):
        k_ref[:, h * slot:h * slot + nope] = kn[:, h * nope:(h + 1) * nope].astype(k_ref.dtype)
        k_ref[:, h * slot + nope:(h + 1) * slot] = kr


def kv_proj(c, wk, wv, kr, dims):
    t, kdim = c.shape
    nope, vh, slot = dims.qk_nope, dims.v_head, dims.q_head_pad
    hb = _pick(dims.heads, 8, 1)
    bm = _pick(t, 1024, SUBLANES)
    row = lambda i, j: (i, 0)
    col = lambda i, j: (0, j)
    out = lambda i, j: (i, j)
    vmem = 2 * (bm * kdim * 2 + kdim * hb * (nope + vh) * 2 + bm * LANES * 2
                + bm * hb * (slot + vh) * 2) + bm * hb * (nope + vh) * 4
    return pl.pallas_call(
        functools.partial(_kv_proj_kernel, nope=nope, slot=slot),
        grid=(t // bm, dims.heads // hb),
        in_specs=[pl.BlockSpec((bm, kdim), row),
                  pl.BlockSpec((kdim, hb * nope), col),
                  pl.BlockSpec((kdim, hb * vh), col),
                  pl.BlockSpec((bm, slot - nope), row)],
        out_specs=[pl.BlockSpec((bm, hb * slot), out),
                   pl.BlockSpec((bm, hb * vh), out)],
        out_shape=[jax.ShapeDtypeStruct((t, dims.heads * slot), BF16),
                   jax.ShapeDtypeStruct((t, dims.heads * vh), BF16)],
        compiler_params=_params(("parallel", "parallel"), vmem),
        name="kv_proj",
    )(c, wk, wv, kr)


def _attn_kernel(q_ref, k_ref, v_ref, o_ref, vx_ref, m_ref, acc_ref, *, blk, tk, chains, vh):
    qi = pl.program_id(2)
    rows = blk // chains

    @pl.when(qi == 0)
    def _():
        vx_ref[:, :vh] = v_ref[...]
        vx_ref[:, vh:] = jnp.ones((vx_ref.shape[0], vx_ref.shape[1] - vh), vx_ref.dtype)

    m_ref[...] = jnp.full(m_ref.shape, -jnp.inf, F32)
    acc_ref[...] = jnp.zeros(acc_ref.shape, F32)

    def chain(c, start, n, diagonal):
        r = slice(c * rows, (c + 1) * rows)
        k = k_ref[pl.ds(start, n), :]
        s = lax.dot_general(q_ref[r, :], k, (((1,), (1,)), ((), ())), preferred_element_type=F32)
        if diagonal:
            tail = s[:, n - rows:]
            row = lax.broadcasted_iota(jnp.int32, tail.shape, 0)
            col = lax.broadcasted_iota(jnp.int32, tail.shape, 1)
            tail = jnp.where(col <= row, tail, MASK_VALUE)
            s = tail if n == rows else jnp.concatenate([s[:, :n - rows], tail], axis=1)
        m_old = m_ref[r, :]
        m_new = jnp.maximum(m_old, jnp.max(s, axis=-1, keepdims=True))
        alpha = jnp.exp2(m_old - m_new)
        p = jnp.exp2(s - m_new).astype(vx_ref.dtype)
        pv = jnp.dot(p, vx_ref[pl.ds(start, n), :], preferred_element_type=F32)
        acc_ref[r, :] = alpha * acc_ref[r, :] + pv
        m_ref[r, :] = m_new

    def body(kb, carry):
        start = pl.multiple_of(kb * tk, tk)
        for c in range(chains):
            chain(c, start, tk, False)
        return carry

    lax.fori_loop(0, qi * (blk // tk), body, 0)
    start = pl.multiple_of(qi * blk, blk)
    for c in range(chains):
        chain(c, start, (c + 1) * rows, True)
    acc = acc_ref[...]
    o_ref[...] = (acc[:, :vh] / acc[:, vh:]).astype(o_ref.dtype)


def attention(q, k, v, dims):
    b, s, h = dims.batch, dims.seq, dims.heads
    slot, vh = dims.q_head_pad, dims.v_head
    blk = _pick(s, 1024, 2 * LANES)
    chains = 4
    tk = blk
    nq = s // blk
    vmem = (2 * (blk * slot * 2 + s * slot * 2 + s * vh * 2 + blk * vh * 2) + s * 2 * vh * 2
            + blk * 2 * vh * 4 + 2 * blk * blk * 4)
    return pl.pallas_call(
        functools.partial(_attn_kernel, blk=blk, tk=tk, chains=chains, vh=vh),
        grid=(b, h, nq),
        in_specs=[pl.BlockSpec((blk, slot), lambda bi, hi, qi: (bi * nq + qi, hi)),
                  pl.BlockSpec((s, slot), lambda bi, hi, qi: (bi, hi)),
                  pl.BlockSpec((s, vh), lambda bi, hi, qi: (bi, hi))],
        out_specs=pl.BlockSpec((blk, vh), lambda bi, hi, qi: (bi * nq + qi, hi)),
        out_shape=jax.ShapeDtypeStruct((b * s, h * vh), BF16),
        scratch_shapes=[pltpu.VMEM((s, 2 * vh), BF16),
                        pltpu.VMEM((blk, 1), F32),
                        pltpu.VMEM((blk, 2 * vh), F32)],
        compiler_params=_params(("parallel", "parallel", "arbitrary"), vmem),
        name="attention",
    )(q, k, v)


def _rg_in_kernel(u_ref, wy_ref, wx_ref, y_ref, x_ref):
    u = u_ref[...]
    y = jnp.dot(u, wy_ref[...], preferred_element_type=F32)
    y_ref[...] = jax.nn.gelu(y, approximate=True).astype(y_ref.dtype)
    x_ref[...] = jnp.dot(u, wx_ref[...], preferred_element_type=F32)


def rg_in_proj(u, w_in):
    t, d = u.shape
    w = w_in.shape[1] // 2
    bm = _pick(t, 1024, SUBLANES)
    bn = _pick(w, 512, LANES)
    nj = w // bn
    vmem = 2 * (bm * d * 2 + 2 * d * bn * 2 + bm * bn * 6) + 3 * bm * bn * 4
    return pl.pallas_call(
        _rg_in_kernel,
        grid=(t // bm, nj),
        in_specs=[pl.BlockSpec((bm, d), lambda i, j: (i, 0)),
                  pl.BlockSpec((d, bn), lambda i, j: (0, j)),
                  pl.BlockSpec((d, bn), lambda i, j: (0, j + nj))],
        out_specs=[pl.BlockSpec((bm, bn), lambda i, j: (i, j)),
                   pl.BlockSpec((bm, bn), lambda i, j: (i, j))],
        out_shape=[jax.ShapeDtypeStruct((t, w), BF16), jax.ShapeDtypeStruct((t, w), F32)],
        compiler_params=_params(("parallel", "parallel"), vmem),
        name="rg_in_proj",
    )(u, w_in, w_in)


def _rglru_kernel(x_ref, y_ref, cw_ref, cb_ref, wa_ref, ba_ref, wx_ref, bx_ref, ap_ref, o_ref,
                  xe_ref, a_ref, b_ref, h_ref, *, tc, lru_block, conv_width):
    @pl.when(pl.program_id(2) == 0)
    def _():
        xe_ref[0:SUBLANES, :] = jnp.zeros((SUBLANES, xe_ref.shape[1]), F32)
        h_ref[...] = jnp.zeros(h_ref.shape, F32)

    x = x_ref[...]
    xe_ref[SUBLANES:, :] = x
    xc = x * cw_ref[conv_width - 1:conv_width, :] + cb_ref[...]
    for d in range(1, conv_width):
        xc = xc + xe_ref[SUBLANES - d:SUBLANES - d + tc, :] * cw_ref[conv_width - 1 - d:conv_width - d, :]
    xe_ref[0:SUBLANES, :] = x[tc - SUBLANES:, :]

    neg_c_softplus = -LRU_C * jax.nn.softplus(-ap_ref[...])
    for n in range(x.shape[1] // lru_block):
        sl = slice(n * lru_block, (n + 1) * lru_block)
        xb = xc[:, sl]
        xb16 = xb.astype(BF16)
        gate_r = jax.nn.sigmoid(jnp.dot(xb16, wa_ref[n], preferred_element_type=F32) + ba_ref[:, sl])
        gate_i = jax.nn.sigmoid(jnp.dot(xb16, wx_ref[n], preferred_element_type=F32) + bx_ref[:, sl])
        log_a = gate_r * neg_c_softplus[:, sl]
        a = jnp.exp(log_a)
        a_ref[:, sl] = a
        b_ref[:, sl] = jnp.sqrt(-jnp.tanh(log_a) * (a * a + 1.0)) * (gate_i * xb)

    row = lax.broadcasted_iota(jnp.int32, (SUBLANES, x.shape[1]), 0)

    def group(g, h_prev):
        start = pl.multiple_of(g * SUBLANES, SUBLANES)
        a = a_ref[pl.ds(start, SUBLANES), :]
        b = b_ref[pl.ds(start, SUBLANES), :]
        d = 1
        while d < SUBLANES:
            keep = row >= d
            b = jnp.where(keep, a * pltpu.roll(b, d, axis=0) + b, b)
            a = jnp.where(keep, a * pltpu.roll(a, d, axis=0), a)
            d *= 2
        h = a * h_prev + b
        b_ref[pl.ds(start, SUBLANES), :] = h
        return jnp.broadcast_to(h[SUBLANES - 1:SUBLANES, :], h.shape)

    h_ref[...] = lax.fori_loop(0, tc // SUBLANES, group, h_ref[...], unroll=4)
    o_ref[...] = (b_ref[...] * y_ref[...].astype(F32)).astype(o_ref.dtype)


def rglru_core(x, y, conv_w, conv_b, wa, ba, wx, bx, a_param, dims):
    b, s, w = dims.batch, dims.seq, dims.lru_width
    lb = dims.lru_block
    cw = _pick(w, 2 * lb, lb)
    tc = _pick(s, 512, SUBLANES)
    nt = s // tc
    npb = cw // lb
    tile = lambda bi, ci, ti: (bi * nt + ti, ci)
    chan = lambda bi, ci, ti: (0, ci)
    wblk = lambda bi, ci, ti: (ci, 0, 0)
    vmem = 2 * (tc * cw * (4 + 2 + 2) + 2 * npb * lb * lb * 2) + 8 * tc * cw * 4
    return pl.pallas_call(
        functools.partial(_rglru_kernel, tc=tc, lru_block=lb, conv_width=dims.conv_width),
        grid=(b, w // cw, nt),
        in_specs=[pl.BlockSpec((tc, cw), tile),
                  pl.BlockSpec((tc, cw), tile),
                  pl.BlockSpec((dims.conv_width, cw), chan),
                  pl.BlockSpec((1, cw), chan),
                  pl.BlockSpec((npb, lb, lb), wblk),
                  pl.BlockSpec((1, cw), chan),
                  pl.BlockSpec((npb, lb, lb), wblk),
                  pl.BlockSpec((1, cw), chan),
                  pl.BlockSpec((1, cw), chan)],
        out_specs=pl.BlockSpec((tc, cw), tile),
        out_shape=jax.ShapeDtypeStruct((b * s, w), BF16),
        scratch_shapes=[pltpu.VMEM((tc + SUBLANES, cw), F32),
                        pltpu.VMEM((tc, cw), F32),
                        pltpu.VMEM((tc, cw), F32),
                        pltpu.VMEM((SUBLANES, cw), F32)],
        compiler_params=_params(("parallel", "parallel", "arbitrary"), vmem),
        name="rglru_core",
    )(x, y, conv_w, conv_b.reshape(1, w), wa, ba.reshape(1, w), wx, bx.reshape(1, w),
      a_param.reshape(1, w))


def _ffn(h, g, w_in, w_out):
    act = swiglu_in(rmsnorm(h, g, BF16), w_in.astype(BF16))
    return matmul_residual(act, w_out.astype(BF16), h, 0.5)


def _pad_heads(w, dims, width):
    k = w.shape[0]
    w = w.reshape(k, dims.heads, width)
    w = jnp.pad(w, ((0, 0), (0, 0), (0, dims.q_head_pad - width)))
    return w.reshape(k, dims.heads * dims.q_head_pad)


def _mla(h, g, tables, w_in, q_norm, kv_norm, w_uq, w_ukv, w_o, dims):
    u = rmsnorm(h, g, BF16)
    ql, kvl, rope = dims.q_lora, dims.kv_lora, dims.qk_rope
    slot = dims.q_head_pad
    w_in = w_in.astype(BF16)
    w_kr = jnp.pad(w_in[:, ql + kvl:], ((0, 0), (0, slot - dims.qk_nope - rope)))
    cq, ckv, kr = mla_in_proj(u, w_in[:, :ql], w_in[:, ql:ql + kvl], w_kr, q_norm, kv_norm, tables,
                              rope // 2)
    scale = (dims.qk_nope + rope) ** -0.5 * math.log2(math.e)
    q = q_proj(cq, _pad_heads(w_uq.astype(BF16), dims, dims.qk_nope + rope), tables, dims, scale)
    w_ukv = w_ukv.astype(BF16).reshape(kvl, dims.heads, dims.qk_nope + dims.v_head)
    w_k = w_ukv[:, :, :dims.qk_nope].reshape(kvl, dims.heads * dims.qk_nope)
    w_v = w_ukv[:, :, dims.qk_nope:].reshape(kvl, dims.heads * dims.v_head)
    k, v = kv_proj(ckv, w_k, w_v, kr, dims)
    o = attention(q, k, v, dims)
    return matmul_residual(o, w_o.astype(BF16), h, 1.0)


def _rglru(h, g, w_in, conv_w, conv_b, wa, ba, wx, bx, a_param, w_out, dims):
    u = rmsnorm(h, g, BF16)
    y, x = rg_in_proj(u, w_in.astype(BF16))
    hy = rglru_core(x, y, conv_w, conv_b, wa.astype(BF16), ba, wx.astype(BF16), bx, a_param, dims)
    return matmul_residual(hy, w_out.astype(BF16), h, 1.0)


def _forward(dims, x, positions, norm_ffn1, ffn1_in, ffn1_out, norm_mix, norm_ffn2, ffn2_in, ffn2_out,
             mla_in, mla_q_norm, mla_kv_norm, mla_w_uq, mla_w_ukv, mla_w_o,
             rg_in, rg_conv_w, rg_conv_b, rg_gate_a_w, rg_gate_a_b, rg_gate_x_w, rg_gate_x_b,
             rg_a_param, rg_out, norm_final):
    depth = norm_ffn1.shape[0]
    h = x.reshape(dims.tokens, dims.d_model)
    tables = rope_tables(positions, dims.qk_rope)
    for i in range(depth):
        h = _ffn(h, norm_ffn1[i], ffn1_in[i], ffn1_out[i])
        j = i // 2
        if i % 2 == 0:
            h = _mla(h, norm_mix[i], tables, mla_in[j], mla_q_norm[j], mla_kv_norm[j], mla_w_uq[j],
                     mla_w_ukv[j], mla_w_o[j], dims)
        else:
            h = _rglru(h, norm_mix[i], rg_in[j], rg_conv_w[j], rg_conv_b[j], rg_gate_a_w[j],
                       rg_gate_a_b[j].reshape(-1), rg_gate_x_w[j], rg_gate_x_b[j].reshape(-1),
                       rg_a_param[j], rg_out[j], dims)
        h = _ffn(h, norm_ffn2[i], ffn2_in[i], ffn2_out[i])
    out = rmsnorm(h, norm_final, x.dtype)
    return out.reshape(x.shape)


def kernel(x, positions, norm_ffn1, ffn1_in, ffn1_out, norm_mix, norm_ffn2, ffn2_in, ffn2_out, mla_in, mla_q_norm, mla_kv_norm, mla_w_uq, mla_w_ukv, mla_w_o, rg_in, rg_conv_w, rg_conv_b, rg_gate_a_w, rg_gate_a_b, rg_gate_x_w, rg_gate_x_b, rg_a_param, rg_out, norm_final):
    batch, seq, d_model = x.shape
    heads = MLA_HEADS
    v_head = mla_w_o.shape[1] // heads
    qk_nope = mla_w_ukv.shape[2] // heads - v_head
    dims = Dims(
        batch=batch, seq=seq, d_model=d_model, d_ff=ffn1_out.shape[1], heads=heads,
        qk_nope=qk_nope, qk_rope=mla_w_uq.shape[2] // heads - qk_nope, v_head=v_head,
        q_lora=mla_q_norm.shape[1], kv_lora=mla_kv_norm.shape[1],
        lru_width=rg_a_param.shape[1], lru_blocks=rg_gate_a_w.shape[1],
        conv_width=rg_conv_w.shape[1])
    return _forward(dims, x, positions, norm_ffn1, ffn1_in, ffn1_out, norm_mix, norm_ffn2, ffn2_in,
                    ffn2_out, mla_in, mla_q_norm, mla_kv_norm, mla_w_uq, mla_w_ukv, mla_w_o,
                    rg_in, rg_conv_w, rg_conv_b, rg_gate_a_w, rg_gate_a_b, rg_gate_x_w, rg_gate_x_b,
                    rg_a_param, rg_out, norm_final)
```

```python
import dataclasses
import functools
import math

import jax
import jax.numpy as jnp
from jax import lax
from jax.experimental import pallas as pl
from jax.experimental.pallas import tpu as pltpu

F32 = jnp.float32
BF16 = jnp.bfloat16

LANES = 128
SUBLANES = 8
V7X_VMEM_BYTES = 64 * 1024 * 1024
VMEM_BUDGET = V7X_VMEM_BYTES - 8 * 1024 * 1024

MLA_HEADS = 32
NORM_EPS = 1e-6
ROPE_THETA = 10000.0
LRU_C = 8.0
MASK_VALUE = -1e30


@dataclasses.dataclass(frozen=True)
class Dims:
    batch: int
    seq: int
    d_model: int
    d_ff: int
    heads: int
    qk_nope: int
    qk_rope: int
    v_head: int
    q_lora: int
    kv_lora: int
    lru_width: int
    lru_blocks: int
    conv_width: int

    @property
    def tokens(self):
        return self.batch * self.seq

    @property
    def lru_block(self):
        return self.lru_width // self.lru_blocks

    @property
    def q_head_pad(self):
        return _round_up(self.qk_nope + self.qk_rope, LANES)


def _round_up(x, m):
    return (x + m - 1) // m * m


def _pick(n, target, quantum):
    if n <= target:
        return n
    best = None
    for d in range(quantum, target + 1, quantum):
        if n % d == 0:
            best = d
    if best is None:
        raise ValueError(f"no block of multiple {quantum} <= {target} divides {n}")
    return best


def _params(semantics, vmem_bytes):
    limit = min(max(int(vmem_bytes * 1.25) + (4 << 20), 32 << 20), VMEM_BUDGET)
    return pltpu.CompilerParams(dimension_semantics=semantics, vmem_limit_bytes=limit)


def _rms(x, g):
    y = x * lax.rsqrt(jnp.mean(x * x, axis=-1, keepdims=True) + NORM_EPS)
    return y * g


def _rmsnorm_kernel(x_ref, g_ref, o_ref):
    o_ref[...] = _rms(x_ref[...], g_ref[...]).astype(o_ref.dtype)


def rmsnorm(x, g, out_dtype):
    t, d = x.shape
    bm = _pick(t, 512, SUBLANES)
    vmem = 2 * bm * d * (4 + jnp.dtype(out_dtype).itemsize)
    return pl.pallas_call(
        _rmsnorm_kernel,
        grid=(t // bm,),
        in_specs=[pl.BlockSpec((bm, d), lambda i: (i, 0)),
                  pl.BlockSpec((1, d), lambda i: (0, 0))],
        out_specs=pl.BlockSpec((bm, d), lambda i: (i, 0)),
        out_shape=jax.ShapeDtypeStruct((t, d), out_dtype),
        compiler_params=_params(("parallel",), vmem),
        name="rmsnorm",
    )(x, g.reshape(1, d))


def _swiglu_kernel(x_ref, wg_ref, wu_ref, o_ref):
    x = x_ref[...]
    g = jnp.dot(x, wg_ref[...], preferred_element_type=F32)
    u = jnp.dot(x, wu_ref[...], preferred_element_type=F32)
    o_ref[...] = (jax.nn.silu(g) * u).astype(o_ref.dtype)


def swiglu_in(x, w_in):
    t, d = x.shape
    f = w_in.shape[1] // 2
    bm = _pick(t, 1024, SUBLANES)
    bn = _pick(f, 512, LANES)
    nj = f // bn
    vmem = 2 * (bm * d * 2 + 2 * d * bn * 2 + bm * bn * 2) + 3 * bm * bn * 4
    return pl.pallas_call(
        _swiglu_kernel,
        grid=(t // bm, nj),
        in_specs=[pl.BlockSpec((bm, d), lambda i, j: (i, 0)),
                  pl.BlockSpec((d, bn), lambda i, j: (0, j)),
                  pl.BlockSpec((d, bn), lambda i, j: (0, j + nj))],
        out_specs=pl.BlockSpec((bm, bn), lambda i, j: (i, j)),
        out_shape=jax.ShapeDtypeStruct((t, f), BF16),
        compiler_params=_params(("parallel", "parallel"), vmem),
        name="swiglu_in",
    )(x, w_in, w_in)


def _mm_res_kernel(a_ref, w_ref, r_ref, o_ref, *, scale):
    @pl.when(pl.program_id(2) == 0)
    def _():
        o_ref[...] = r_ref[...]

    part = jnp.dot(a_ref[...], w_ref[...], preferred_element_type=F32)
    if scale != 1.0:
        part = scale * part
    o_ref[...] += part


def matmul_residual(a, w, res, scale):
    t, kdim = a.shape
    n = w.shape[1]
    bm = _pick(t, 1024, SUBLANES)
    bn = _pick(n, 2048, LANES)
    bk = _pick(kdim, 1024, LANES)
    vmem = 2 * (bm * bk * 2 + bk * bn * 2 + 2 * bm * bn * 4) + bm * bn * 4
    return pl.pallas_call(
        functools.partial(_mm_res_kernel, scale=scale),
        grid=(t // bm, n // bn, kdim // bk),
        in_specs=[pl.BlockSpec((bm, bk), lambda i, j, k: (i, k)),
                  pl.BlockSpec((bk, bn), lambda i, j, k: (k, j)),
                  pl.BlockSpec((bm, bn), lambda i, j, k: (i, j))],
        out_specs=pl.BlockSpec((bm, bn), lambda i, j, k: (i, j)),
        out_shape=jax.ShapeDtypeStruct((t, n), F32),
        compiler_params=_params(("parallel", "parallel", "arbitrary"), vmem),
        name="matmul_residual",
    )(a, w, res)


def _rope_table_kernel(pos_ref, freq_ref, cos_ref, sina_ref, sinb_ref, *, half):
    ang = pos_ref[...].astype(F32) * freq_ref[...]
    lane = lax.broadcasted_iota(jnp.int32, ang.shape, 1)
    c = jnp.cos(ang)
    s = jnp.sin(ang)
    cos_ref[...] = jnp.where(lane < 2 * half, c, 0.0)
    sina_ref[...] = jnp.where(lane < half, -s, 0.0)
    sinb_ref[...] = jnp.where((lane >= half) & (lane < 2 * half), s, 0.0)


def rope_tables(positions, rope_dim):
    half = rope_dim // 2
    t = positions.size
    inv_freq = ROPE_THETA ** (-jnp.arange(half, dtype=F32) / half)
    freq_row = jnp.zeros((1, LANES), F32).at[0, :half].set(inv_freq).at[0, half:2 * half].set(inv_freq)
    bm = _pick(t, 1024, SUBLANES)
    out = jax.ShapeDtypeStruct((t, LANES), F32)
    spec = pl.BlockSpec((bm, LANES), lambda i: (i, 0))
    return pl.pallas_call(
        functools.partial(_rope_table_kernel, half=half),
        grid=(t // bm,),
        in_specs=[pl.BlockSpec((bm, 1), lambda i: (i, 0)),
                  pl.BlockSpec((1, LANES), lambda i: (0, 0))],
        out_specs=[spec, spec, spec],
        out_shape=[out, out, out],
        compiler_params=_params(("parallel",), 8 * bm * LANES * 4),
        name="rope_tables",
    )(positions.reshape(t, 1), freq_row)


def _rope_slot(x, cos, sina, sinb, half):
    x2_on_x1 = pltpu.roll(x, LANES - half, axis=1)
    x1_on_x2 = pltpu.roll(x, half, axis=1)
    return x * cos + x2_on_x1 * sina + x1_on_x2 * sinb


def _mla_in_kernel(u_ref, wq_ref, wkv_ref, wkr_ref, gq_ref, gkv_ref, cos_ref, sina_ref, sinb_ref,
                   cq_ref, ckv_ref, kr_ref, *, half):
    u = u_ref[...]
    cq = jnp.dot(u, wq_ref[...], preferred_element_type=F32)
    cq_ref[...] = _rms(cq, gq_ref[...]).astype(cq_ref.dtype)
    ckv = jnp.dot(u, wkv_ref[...], preferred_element_type=F32)
    ckv_ref[...] = _rms(ckv, gkv_ref[...]).astype(ckv_ref.dtype)
    kr = jnp.dot(u, wkr_ref[...], preferred_element_type=F32)
    kr_ref[...] = _rope_slot(kr, cos_ref[...], sina_ref[...], sinb_ref[...], half).astype(kr_ref.dtype)


def mla_in_proj(u, wq, wkv, wkr, gq, gkv, tables, half):
    t, d = u.shape
    nq, nkv = wq.shape[1], wkv.shape[1]
    bm = _pick(t, 512, SUBLANES)
    row = lambda i: (i, 0)
    fixed = lambda i: (0, 0)
    vmem = 2 * (bm * d * 2 + d * (nq + nkv + LANES) * 2 + bm * (nq + nkv + LANES) * 2
                + 3 * bm * LANES * 4) + 2 * bm * (nq + nkv) * 4
    return pl.pallas_call(
        functools.partial(_mla_in_kernel, half=half),
        grid=(t // bm,),
        in_specs=[pl.BlockSpec((bm, d), row),
                  pl.BlockSpec((d, nq), fixed),
                  pl.BlockSpec((d, nkv), fixed),
                  pl.BlockSpec((d, LANES), fixed),
                  pl.BlockSpec((1, nq), fixed),
                  pl.BlockSpec((1, nkv), fixed),
                  pl.BlockSpec((bm, LANES), row),
                  pl.BlockSpec((bm, LANES), row),
                  pl.BlockSpec((bm, LANES), row)],
        out_specs=[pl.BlockSpec((bm, nq), row),
                   pl.BlockSpec((bm, nkv), row),
                   pl.BlockSpec((bm, LANES), row)],
        out_shape=[jax.ShapeDtypeStruct((t, nq), BF16),
                   jax.ShapeDtypeStruct((t, nkv), BF16),
                   jax.ShapeDtypeStruct((t, LANES), BF16)],
        compiler_params=_params(("parallel",), vmem),
        name="mla_in_proj",
    )(u, wq, wkv, wkr, gq.reshape(1, nq), gkv.reshape(1, nkv), *tables)


def _q_proj_kernel(c_ref, w_ref, cos_ref, sina_ref, sinb_ref, o_ref, *, half, nope, slot, scale):
    q = jnp.dot(c_ref[...], w_ref[...], preferred_element_type=F32)
    cos, sina, sinb = cos_ref[...], sina_ref[...], sinb_ref[...]
    for h in range(q.shape[1] // slot):
        base = h * slot
        o_ref[:, base:base + nope] = (q[:, base:base + nope] * scale).astype(o_ref.dtype)
        r = _rope_slot(q[:, base + nope:base + slot], cos, sina, sinb, half)
        o_ref[:, base + nope:base + slot] = (r * scale).astype(o_ref.dtype)


def q_proj(c, w, tables, dims, scale):
    t, kdim = c.shape
    n = w.shape[1]
    slot = dims.q_head_pad
    bm = _pick(t, 1024, SUBLANES)
    bn = _pick(n, 4 * slot, slot)
    row = lambda i, j: (i, 0)
    vmem = 2 * (bm * kdim * 2 + kdim * bn * 2 + bm * bn * 2 + 3 * bm * LANES * 4) + 2 * bm * bn * 4
    return pl.pallas_call(
        functools.partial(_q_proj_kernel, half=dims.qk_rope // 2, nope=dims.qk_nope, slot=slot,
                          scale=scale),
        grid=(t // bm, n // bn),
        in_specs=[pl.BlockSpec((bm, kdim), row),
                  pl.BlockSpec((kdim, bn), lambda i, j: (0, j)),
                  pl.BlockSpec((bm, LANES), row),
                  pl.BlockSpec((bm, LANES), row),
                  pl.BlockSpec((bm, LANES), row)],
        out_specs=pl.BlockSpec((bm, bn), lambda i, j: (i, j)),
        out_shape=jax.ShapeDtypeStruct((t, n), BF16),
        compiler_params=_params(("parallel", "parallel"), vmem),
        name="q_proj",
    )(c, w, *tables)


def _kv_proj_kernel(c_ref, wk_ref, wv_ref, kr_ref, k_ref, v_ref, *, nope, slot):
    c = c_ref[...]
    kn = jnp.dot(c, wk_ref[...], preferred_element_type=F32)
    v_ref[...] = jnp.dot(c, wv_ref[...], preferred_element_type=F32).astype(v_ref.dtype)
    kr = kr_ref[...]
    for h in range(kn.shape[1] // nope):
        k_ref[:, h * slot:h * slot + nope] = kn[:, h * nope:(h + 1) * nope].astype(k_ref.dtype)
        k_ref[:, h * slot + nope:(h + 1) * slot] = kr


def kv_proj(c, wk, wv, kr, dims):
    t, kdim = c.shape
    nope, vh, slot = dims.qk_nope, dims.v_head, dims.q_head_pad
    hb = _pick(dims.heads, 8, 1)
    bm = _pick(t, 1024, SUBLANES)
    row = lambda i, j: (i, 0)
    col = lambda i, j: (0, j)
    out = lambda i, j: (i, j)
    vmem = 2 * (bm * kdim * 2 + kdim * hb * (nope + vh) * 2 + bm * LANES * 2
                + bm * hb * (slot + vh) * 2) + bm * hb * (nope + vh) * 4
    return pl.pallas_call(
        functools.partial(_kv_proj_kernel, nope=nope, slot=slot),
        grid=(t // bm, dims.heads // hb),
        in_specs=[pl.BlockSpec((bm, kdim), row),
                  pl.BlockSpec((kdim, hb * nope), col),
                  pl.BlockSpec((kdim, hb * vh), col),
                  pl.BlockSpec((bm, slot - nope), row)],
        out_specs=[pl.BlockSpec((bm, hb * slot), out),
                   pl.BlockSpec((bm, hb * vh), out)],
        out_shape=[jax.ShapeDtypeStruct((t, dims.heads * slot), BF16),
                   jax.ShapeDtypeStruct((t, dims.heads * vh), BF16)],
        compiler_params=_params(("parallel", "parallel"), vmem),
        name="kv_proj",
    )(c, wk, wv, kr)


def _attn_kernel(q_ref, k_ref, v_ref, o_ref, vx_ref, m_ref, acc_ref, *, blk, tk, chains, vh):
    qi = pl.program_id(2)
    rows = blk // chains

    @pl.when(qi == 0)
    def _():
        vx_ref[:, :vh] = v_ref[...]
        vx_ref[:, vh:] = jnp.ones((vx_ref.shape[0], vx_ref.shape[1] - vh), vx_ref.dtype)

    m_ref[...] = jnp.full(m_ref.shape, -jnp.inf, F32)
    acc_ref[...] = jnp.zeros(acc_ref.shape, F32)

    def chain(c, start, n, diagonal):
        r = slice(c * rows, (c + 1) * rows)
        k = k_ref[pl.ds(start, n), :]
        s = lax.dot_general(q_ref[r, :], k, (((1,), (1,)), ((), ())), preferred_element_type=F32)
        if diagonal:
            tail = s[:, n - rows:]
            row = lax.broadcasted_iota(jnp.int32, tail.shape, 0)
            col = lax.broadcasted_iota(jnp.int32, tail.shape, 1)
            tail = jnp.where(col <= row, tail, MASK_VALUE)
            s = tail if n == rows else jnp.concatenate([s[:, :n - rows], tail], axis=1)
        m_old = m_ref[r, :]
        m_new = jnp.maximum(m_old, jnp.max(s, axis=-1, keepdims=True))
        alpha = jnp.exp2(m_old - m_new)
        p = jnp.exp2(s - m_new).astype(vx_ref.dtype)
        pv = jnp.dot(p, vx_ref[pl.ds(start, n), :], preferred_element_type=F32)
        acc_ref[r, :] = alpha * acc_ref[r, :] + pv
        m_ref[r, :] = m_new

    def full_block(kb):
        start = pl.multiple_of(kb * tk, tk)
        for c in range(chains):
            chain(c, start, tk, False)

    def body(pair, carry):
        full_block(2 * pair)
        full_block(2 * pair + 1)
        return carry

    n_full = qi * (blk // tk)
    lax.fori_loop(0, n_full // 2, body, 0)

    def diagonal_block():
        start = pl.multiple_of(qi * blk, blk)
        for c in range(chains):
            chain(c, start, (c + 1) * rows, True)

    @pl.when(n_full % 2 == 1)
    def _():
        full_block(n_full - 1)
        diagonal_block()

    @pl.when(n_full % 2 == 0)
    def _():
        diagonal_block()

    acc = acc_ref[...]
    o_ref[...] = (acc[:, :vh] / acc[:, vh:]).astype(o_ref.dtype)


def attention(q, k, v, dims):
    b, s, h = dims.batch, dims.seq, dims.heads
    slot, vh = dims.q_head_pad, dims.v_head
    blk = _pick(s, 1024, 2 * LANES)
    chains = 4
    tk = blk
    nq = s // blk
    vmem = (2 * (blk * slot * 2 + s * slot * 2 + s * vh * 2 + blk * vh * 2) + s * 2 * vh * 2
            + blk * 2 * vh * 4 + 2 * blk * blk * 4)
    return pl.pallas_call(
        functools.partial(_attn_kernel, blk=blk, tk=tk, chains=chains, vh=vh),
        grid=(b, h, nq),
        in_specs=[pl.BlockSpec((blk, slot), lambda bi, hi, qi: (bi * nq + qi, hi)),
                  pl.BlockSpec((s, slot), lambda bi, hi, qi: (bi, hi)),
                  pl.BlockSpec((s, vh), lambda bi, hi, qi: (bi, hi))],
        out_specs=pl.BlockSpec((blk, vh), lambda bi, hi, qi: (bi * nq + qi, hi)),
        out_shape=jax.ShapeDtypeStruct((b * s, h * vh), BF16),
        scratch_shapes=[pltpu.VMEM((s, 2 * vh), BF16),
                        pltpu.VMEM((blk, 1), F32),
                        pltpu.VMEM((blk, 2 * vh), F32)],
        compiler_params=_params(("parallel", "parallel", "arbitrary"), vmem),
        name="attention",
    )(q, k, v)


def _rg_in_kernel(u_ref, wy_ref, wx_ref, y_ref, x_ref):
    u = u_ref[...]
    y = jnp.dot(u, wy_ref[...], preferred_element_type=F32)
    y_ref[...] = jax.nn.gelu(y, approximate=True).astype(y_ref.dtype)
    x_ref[...] = jnp.dot(u, wx_ref[...], preferred_element_type=F32)


def rg_in_proj(u, w_in):
    t, d = u.shape
    w = w_in.shape[1] // 2
    bm = _pick(t, 1024, SUBLANES)
    bn = _pick(w, 512, LANES)
    nj = w // bn
    vmem = 2 * (bm * d * 2 + 2 * d * bn * 2 + bm * bn * 6) + 3 * bm * bn * 4
    return pl.pallas_call(
        _rg_in_kernel,
        grid=(t // bm, nj),
        in_specs=[pl.BlockSpec((bm, d), lambda i, j: (i, 0)),
                  pl.BlockSpec((d, bn), lambda i, j: (0, j)),
                  pl.BlockSpec((d, bn), lambda i, j: (0, j + nj))],
        out_specs=[pl.BlockSpec((bm, bn), lambda i, j: (i, j)),
                   pl.BlockSpec((bm, bn), lambda i, j: (i, j))],
        out_shape=[jax.ShapeDtypeStruct((t, w), BF16), jax.ShapeDtypeStruct((t, w), F32)],
        compiler_params=_params(("parallel", "parallel"), vmem),
        name="rg_in_proj",
    )(u, w_in, w_in)


def _rglru_kernel(x_ref, y_ref, cw_ref, cb_ref, wa_ref, ba_ref, wx_ref, bx_ref, ap_ref, o_ref,
                  xe_ref, a_ref, b_ref, h_ref, *, tc, lru_block, conv_width):
    @pl.when(pl.program_id(2) == 0)
    def _():
        xe_ref[0:SUBLANES, :] = jnp.zeros((SUBLANES, xe_ref.shape[1]), F32)
        h_ref[...] = jnp.zeros(h_ref.shape, F32)

    x = x_ref[...]
    xe_ref[SUBLANES:, :] = x
    xc = x * cw_ref[conv_width - 1:conv_width, :] + cb_ref[...]
    for d in range(1, conv_width):
        xc = xc + xe_ref[SUBLANES - d:SUBLANES - d + tc, :] * cw_ref[conv_width - 1 - d:conv_width - d, :]
    xe_ref[0:SUBLANES, :] = x[tc - SUBLANES:, :]

    neg_c_softplus = -LRU_C * jax.nn.softplus(-ap_ref[...])
    for n in range(x.shape[1] // lru_block):
        sl = slice(n * lru_block, (n + 1) * lru_block)
        xb = xc[:, sl]
        xb16 = xb.astype(BF16)
        gate_r = jax.nn.sigmoid(jnp.dot(xb16, wa_ref[n], preferred_element_type=F32) + ba_ref[:, sl])
        gate_i = jax.nn.sigmoid(jnp.dot(xb16, wx_ref[n], preferred_element_type=F32) + bx_ref[:, sl])
        log_a = gate_r * neg_c_softplus[:, sl]
        a = jnp.exp(log_a)
        a_ref[:, sl] = a
        b_ref[:, sl] = jnp.sqrt(-jnp.tanh(log_a) * (a * a + 1.0)) * (gate_i * xb)

    row = lax.broadcasted_iota(jnp.int32, (SUBLANES, x.shape[1]), 0)

    def group(g, h_prev):
        start = pl.multiple_of(g * SUBLANES, SUBLANES)
        a = a_ref[pl.ds(start, SUBLANES), :]
        b = b_ref[pl.ds(start, SUBLANES), :]
        d = 1
        while d < SUBLANES:
            keep = row >= d
            b = jnp.where(keep, a * pltpu.roll(b, d, axis=0) + b, b)
            a = jnp.where(keep, a * pltpu.roll(a, d, axis=0), a)
            d *= 2
        h = a * h_prev + b
        b_ref[pl.ds(start, SUBLANES), :] = h
        return jnp.broadcast_to(h[SUBLANES - 1:SUBLANES, :], h.shape)

    h_ref[...] = lax.fori_loop(0, tc // SUBLANES, group, h_ref[...], unroll=4)
    o_ref[...] = (b_ref[...] * y_ref[...].astype(F32)).astype(o_ref.dtype)


def rglru_core(x, y, conv_w, conv_b, wa, ba, wx, bx, a_param, dims):
    b, s, w = dims.batch, dims.seq, dims.lru_width
    lb = dims.lru_block
    cw = _pick(w, 2 * lb, lb)
    tc = _pick(s, 512, SUBLANES)
    nt = s // tc
    npb = cw // lb
    tile = lambda bi, ci, ti: (bi * nt + ti, ci)
    chan = lambda bi, ci, ti: (0, ci)
    wblk = lambda bi, ci, ti: (ci, 0, 0)
    vmem = 2 * (tc * cw * (4 + 2 + 2) + 2 * npb * lb * lb * 2) + 8 * tc * cw * 4
    return pl.pallas_call(
        functools.partial(_rglru_kernel, tc=tc, lru_block=lb, conv_width=dims.conv_width),
        grid=(b, w // cw, nt),
        in_specs=[pl.BlockSpec((tc, cw), tile),
                  pl.BlockSpec((tc, cw), tile),
                  pl.BlockSpec((dims.conv_width, cw), chan),
                  pl.BlockSpec((1, cw), chan),
                  pl.BlockSpec((npb, lb, lb), wblk),
                  pl.BlockSpec((1, cw), chan),
                  pl.BlockSpec((npb, lb, lb), wblk),
                  pl.BlockSpec((1, cw), chan),
                  pl.BlockSpec((1, cw), chan)],
        out_specs=pl.BlockSpec((tc, cw), tile),
        out_shape=jax.ShapeDtypeStruct((b * s, w), BF16),
        scratch_shapes=[pltpu.VMEM((tc + SUBLANES, cw), F32),
                        pltpu.VMEM((tc, cw), F32),
                        pltpu.VMEM((tc, cw), F32),
                        pltpu.VMEM((SUBLANES, cw), F32)],
        compiler_params=_params(("parallel", "parallel", "arbitrary"), vmem),
        name="rglru_core",
    )(x, y, conv_w, conv_b.reshape(1, w), wa, ba.reshape(1, w), wx, bx.reshape(1, w),
      a_param.reshape(1, w))


def _ffn(h, g, w_in, w_out):
    act = swiglu_in(rmsnorm(h, g, BF16), w_in.astype(BF16))
    return matmul_residual(act, w_out.astype(BF16), h, 0.5)


def _pad_heads(w, dims, width):
    k = w.shape[0]
    w = w.reshape(k, dims.heads, width)
    w = jnp.pad(w, ((0, 0), (0, 0), (0, dims.q_head_pad - width)))
    return w.reshape(k, dims.heads * dims.q_head_pad)


def _mla(h, g, tables, w_in, q_norm, kv_norm, w_uq, w_ukv, w_o, dims):
    u = rmsnorm(h, g, BF16)
    ql, kvl, rope = dims.q_lora, dims.kv_lora, dims.qk_rope
    slot = dims.q_head_pad
    w_in = w_in.astype(BF16)
    w_kr = jnp.pad(w_in[:, ql + kvl:], ((0, 0), (0, slot - dims.qk_nope - rope)))
    cq, ckv, kr = mla_in_proj(u, w_in[:, :ql], w_in[:, ql:ql + kvl], w_kr, q_norm, kv_norm, tables,
                              rope // 2)
    scale = (dims.qk_nope + rope) ** -0.5 * math.log2(math.e)
    q = q_proj(cq, _pad_heads(w_uq.astype(BF16), dims, dims.qk_nope + rope), tables, dims, scale)
    w_ukv = w_ukv.astype(BF16).reshape(kvl, dims.heads, dims.qk_nope + dims.v_head)
    w_k = w_ukv[:, :, :dims.qk_nope].reshape(kvl, dims.heads * dims.qk_nope)
    w_v = w_ukv[:, :, dims.qk_nope:].reshape(kvl, dims.heads * dims.v_head)
    k, v = kv_proj(ckv, w_k, w_v, kr, dims)
    o = attention(q, k, v, dims)
    return matmul_residual(o, w_o.astype(BF16), h, 1.0)


def _rglru(h, g, w_in, conv_w, conv_b, wa, ba, wx, bx, a_param, w_out, dims):
    u = rmsnorm(h, g, BF16)
    y, x = rg_in_proj(u, w_in.astype(BF16))
    hy = rglru_core(x, y, conv_w, conv_b, wa.astype(BF16), ba, wx.astype(BF16), bx, a_param, dims)
    return matmul_residual(hy, w_out.astype(BF16), h, 1.0)


def _forward(dims, x, positions, norm_ffn1, ffn1_in, ffn1_out, norm_mix, norm_ffn2, ffn2_in, ffn2_out,
             mla_in, mla_q_norm, mla_kv_norm, mla_w_uq, mla_w_ukv, mla_w_o,
             rg_in, rg_conv_w, rg_conv_b, rg_gate_a_w, rg_gate_a_b, rg_gate_x_w, rg_gate_x_b,
             rg_a_param, rg_out, norm_final):
    depth = norm_ffn1.shape[0]
    h = x.reshape(dims.tokens, dims.d_model)
    tables = rope_tables(positions, dims.qk_rope)
    for i in range(depth):
        h = _ffn(h, norm_ffn1[i], ffn1_in[i], ffn1_out[i])
        j = i // 2
        if i % 2 == 0:
            h = _mla(h, norm_mix[i], tables, mla_in[j], mla_q_norm[j], mla_kv_norm[j], mla_w_uq[j],
                     mla_w_ukv[j], mla_w_o[j], dims)
        else:
            h = _rglru(h, norm_mix[i], rg_in[j], rg_conv_w[j], rg_conv_b[j], rg_gate_a_w[j],
                       rg_gate_a_b[j].reshape(-1), rg_gate_x_w[j], rg_gate_x_b[j].reshape(-1),
                       rg_a_param[j], rg_out[j], dims)
        h = _ffn(h, norm_ffn2[i], ffn2_in[i], ffn2_out[i])
    out = rmsnorm(h, norm_final, x.dtype)
    return out.reshape(x.shape)


def kernel(x, positions, norm_ffn1, ffn1_in, ffn1_out, norm_mix, norm_ffn2, ffn2_in, ffn2_out, mla_in, mla_q_norm, mla_kv_norm, mla_w_uq, mla_w_ukv, mla_w_o, rg_in, rg_conv_w, rg_conv_b, rg_gate_a_w, rg_gate_a_b, rg_gate_x_w, rg_gate_x_b, rg_a_param, rg_out, norm_final):
    batch, seq, d_model = x.shape
    heads = MLA_HEADS
    v_head = mla_w_o.shape[1] // heads
    qk_nope = mla_w_ukv.shape[2] // heads - v_head
    dims = Dims(
        batch=batch, seq=seq, d_model=d_model, d_ff=ffn1_out.shape[1], heads=heads,
        qk_nope=qk_nope, qk_rope=mla_w_uq.shape[2] // heads - qk_nope, v_head=v_head,
        q_lora=mla_q_norm.shape[1], kv_lora=mla_kv_norm.shape[1],
        lru_width=rg_a_param.shape[1], lru_blocks=rg_gate_a_w.shape[1],
        conv_width=rg_conv_w.shape[1])
    return _forward(dims, x, positions, norm_ffn1, ffn1_in, ffn1_out, norm_mix, norm_ffn2, ffn2_in,
                    ffn2_out, mla_in, mla_q_norm, mla_kv_norm, mla_w_uq, mla_w_ukv, mla_w_o,
                    rg_in, rg_conv_w, rg_conv_b, rg_gate_a_w, rg_gate_a_b, rg_gate_x_w, rg_gate_x_b,
                    rg_a_param, rg_out, norm_final)
```

```python
import dataclasses
import functools
import math

import jax
import jax.numpy as jnp
from jax import lax
from jax.experimental import pallas as pl
from jax.experimental.pallas import tpu as pltpu

F32 = jnp.float32
BF16 = jnp.bfloat16

LANES = 128
SUBLANES = 8
V7X_VMEM_BYTES = 64 * 1024 * 1024
VMEM_BUDGET = V7X_VMEM_BYTES - 8 * 1024 * 1024

MLA_HEADS = 32
NORM_EPS = 1e-6
ROPE_THETA = 10000.0
LRU_C = 8.0
MASK_VALUE = -1e30


@dataclasses.dataclass(frozen=True)
class Dims:
    batch: int
    seq: int
    d_model: int
    d_ff: int
    heads: int
    qk_nope: int
    qk_rope: int
    v_head: int
    q_lora: int
    kv_lora: int
    lru_width: int
    lru_blocks: int
    conv_width: int

    @property
    def tokens(self):
        return self.batch * self.seq

    @property
    def lru_block(self):
        return self.lru_width // self.lru_blocks

    @property
    def q_head_pad(self):
        return _round_up(self.qk_nope + self.qk_rope, LANES)


def _round_up(x, m):
    return (x + m - 1) // m * m


def _pick(n, target, quantum):
    if n <= target:
        return n
    best = None
    for d in range(quantum, target + 1, quantum):
        if n % d == 0:
            best = d
    if best is None:
        raise ValueError(f"no block of multiple {quantum} <= {target} divides {n}")
    return best


def _params(semantics, vmem_bytes):
    limit = min(max(int(vmem_bytes * 1.25) + (4 << 20), 32 << 20), VMEM_BUDGET)
    return pltpu.CompilerParams(dimension_semantics=semantics, vmem_limit_bytes=limit)


def _rms(x, g):
    y = x * lax.rsqrt(jnp.mean(x * x, axis=-1, keepdims=True) + NORM_EPS)
    return y * g


def _rmsnorm_kernel(x_ref, g_ref, o_ref):
    o_ref[...] = _rms(x_ref[...], g_ref[...]).astype(o_ref.dtype)


def rmsnorm(x, g, out_dtype):
    t, d = x.shape
    bm = _pick(t, 512, SUBLANES)
    vmem = 2 * bm * d * (4 + jnp.dtype(out_dtype).itemsize)
    return pl.pallas_call(
        _rmsnorm_kernel,
        grid=(t // bm,),
        in_specs=[pl.BlockSpec((bm, d), lambda i: (i, 0)),
                  pl.BlockSpec((1, d), lambda i: (0, 0))],
        out_specs=pl.BlockSpec((bm, d), lambda i: (i, 0)),
        out_shape=jax.ShapeDtypeStruct((t, d), out_dtype),
        compiler_params=_params(("parallel",), vmem),
        name="rmsnorm",
    )(x, g.reshape(1, d))


def _swiglu_kernel(x_ref, wg_ref, wu_ref, o_ref):
    x = x_ref[...]
    g = jnp.dot(x, wg_ref[...].astype(x.dtype), preferred_element_type=F32)
    u = jnp.dot(x, wu_ref[...].astype(x.dtype), preferred_element_type=F32)
    o_ref[...] = (jax.nn.silu(g) * u).astype(o_ref.dtype)


def swiglu_in(x, w_in):
    t, d = x.shape
    f = w_in.shape[1] // 2
    bm = _pick(t, 1024, SUBLANES)
    bn = _pick(f, 512, LANES)
    nj = f // bn
    wbytes = jnp.dtype(w_in.dtype).itemsize
    vmem = bm * d * 2 + 2 * (2 * d * bn * wbytes + bm * bn * 2) + 2 * d * bn * 2 + 3 * bm * bn * 4
    return pl.pallas_call(
        _swiglu_kernel,
        grid=(t // bm, nj),
        in_specs=[pl.BlockSpec((bm, d), lambda i, j: (i, 0), pipeline_mode=pl.Buffered(1)),
                  pl.BlockSpec((d, bn), lambda i, j: (0, j)),
                  pl.BlockSpec((d, bn), lambda i, j: (0, j + nj))],
        out_specs=pl.BlockSpec((bm, bn), lambda i, j: (i, j)),
        out_shape=jax.ShapeDtypeStruct((t, f), BF16),
        compiler_params=_params(("parallel", "parallel"), vmem),
        name="swiglu_in",
    )(x, w_in, w_in)


def _mm_res_kernel(a_ref, w_ref, r_ref, o_ref, *, scale):
    @pl.when(pl.program_id(2) == 0)
    def _():
        o_ref[...] = r_ref[...]

    part = jnp.dot(a_ref[...], w_ref[...], preferred_element_type=F32)
    if scale != 1.0:
        part = scale * part
    o_ref[...] += part


def matmul_residual(a, w, res, scale):
    t, kdim = a.shape
    n = w.shape[1]
    bm = _pick(t, 1024, SUBLANES)
    bn = _pick(n, 2048, LANES)
    bk = _pick(kdim, 1024, LANES)
    vmem = 2 * (bm * bk * 2 + bk * bn * 2 + 2 * bm * bn * 4) + bm * bn * 4
    return pl.pallas_call(
        functools.partial(_mm_res_kernel, scale=scale),
        grid=(t // bm, n // bn, kdim // bk),
        in_specs=[pl.BlockSpec((bm, bk), lambda i, j, k: (i, k)),
                  pl.BlockSpec((bk, bn), lambda i, j, k: (k, j)),
                  pl.BlockSpec((bm, bn), lambda i, j, k: (i, j))],
        out_specs=pl.BlockSpec((bm, bn), lambda i, j, k: (i, j)),
        out_shape=jax.ShapeDtypeStruct((t, n), F32),
        compiler_params=_params(("parallel", "parallel", "arbitrary"), vmem),
        name="matmul_residual",
    )(a, w, res)


def _rope_table_kernel(pos_ref, freq_ref, cos_ref, sina_ref, sinb_ref, *, half):
    ang = pos_ref[...].astype(F32) * freq_ref[...]
    lane = lax.broadcasted_iota(jnp.int32, ang.shape, 1)
    c = jnp.cos(ang)
    s = jnp.sin(ang)
    cos_ref[...] = jnp.where(lane < 2 * half, c, 0.0)
    sina_ref[...] = jnp.where(lane < half, -s, 0.0)
    sinb_ref[...] = jnp.where((lane >= half) & (lane < 2 * half), s, 0.0)


def rope_tables(positions, rope_dim):
    half = rope_dim // 2
    t = positions.size
    inv_freq = ROPE_THETA ** (-jnp.arange(half, dtype=F32) / half)
    freq_row = jnp.zeros((1, LANES), F32).at[0, :half].set(inv_freq).at[0, half:2 * half].set(inv_freq)
    bm = _pick(t, 1024, SUBLANES)
    out = jax.ShapeDtypeStruct((t, LANES), F32)
    spec = pl.BlockSpec((bm, LANES), lambda i: (i, 0))
    return pl.pallas_call(
        functools.partial(_rope_table_kernel, half=half),
        grid=(t // bm,),
        in_specs=[pl.BlockSpec((bm, 1), lambda i: (i, 0)),
                  pl.BlockSpec((1, LANES), lambda i: (0, 0))],
        out_specs=[spec, spec, spec],
        out_shape=[out, out, out],
        compiler_params=_params(("parallel",), 8 * bm * LANES * 4),
        name="rope_tables",
    )(positions.reshape(t, 1), freq_row)


def _rope_slot(x, cos, sina, sinb, half):
    x2_on_x1 = pltpu.roll(x, LANES - half, axis=1)
    x1_on_x2 = pltpu.roll(x, half, axis=1)
    return x * cos + x2_on_x1 * sina + x1_on_x2 * sinb


def _mla_in_kernel(u_ref, wq_ref, wkv_ref, wkr_ref, gq_ref, gkv_ref, cos_ref, sina_ref, sinb_ref,
                   cq_ref, ckv_ref, kr_ref, *, half):
    u = u_ref[...]
    cq = jnp.dot(u, wq_ref[...], preferred_element_type=F32)
    cq_ref[...] = _rms(cq, gq_ref[...]).astype(cq_ref.dtype)
    ckv = jnp.dot(u, wkv_ref[...], preferred_element_type=F32)
    ckv_ref[...] = _rms(ckv, gkv_ref[...]).astype(ckv_ref.dtype)
    kr = jnp.dot(u, wkr_ref[...], preferred_element_type=F32)
    kr_ref[...] = _rope_slot(kr, cos_ref[...], sina_ref[...], sinb_ref[...], half).astype(kr_ref.dtype)


def mla_in_proj(u, wq, wkv, wkr, gq, gkv, tables, half):
    t, d = u.shape
    nq, nkv = wq.shape[1], wkv.shape[1]
    bm = _pick(t, 512, SUBLANES)
    row = lambda i: (i, 0)
    fixed = lambda i: (0, 0)
    vmem = 2 * (bm * d * 2 + d * (nq + nkv + LANES) * 2 + bm * (nq + nkv + LANES) * 2
                + 3 * bm * LANES * 4) + 2 * bm * (nq + nkv) * 4
    return pl.pallas_call(
        functools.partial(_mla_in_kernel, half=half),
        grid=(t // bm,),
        in_specs=[pl.BlockSpec((bm, d), row),
                  pl.BlockSpec((d, nq), fixed),
                  pl.BlockSpec((d, nkv), fixed),
                  pl.BlockSpec((d, LANES), fixed),
                  pl.BlockSpec((1, nq), fixed),
                  pl.BlockSpec((1, nkv), fixed),
                  pl.BlockSpec((bm, LANES), row),
                  pl.BlockSpec((bm, LANES), row),
                  pl.BlockSpec((bm, LANES), row)],
        out_specs=[pl.BlockSpec((bm, nq), row),
                   pl.BlockSpec((bm, nkv), row),
                   pl.BlockSpec((bm, LANES), row)],
        out_shape=[jax.ShapeDtypeStruct((t, nq), BF16),
                   jax.ShapeDtypeStruct((t, nkv), BF16),
                   jax.ShapeDtypeStruct((t, LANES), BF16)],
        compiler_params=_params(("parallel",), vmem),
        name="mla_in_proj",
    )(u, wq, wkv, wkr, gq.reshape(1, nq), gkv.reshape(1, nkv), *tables)


def _q_proj_kernel(c_ref, w_ref, cos_ref, sina_ref, sinb_ref, o_ref, *, half, nope, slot, scale):
    q = jnp.dot(c_ref[...], w_ref[...], preferred_element_type=F32)
    cos, sina, sinb = cos_ref[...], sina_ref[...], sinb_ref[...]
    for h in range(q.shape[1] // slot):
        base = h * slot
        o_ref[:, base:base + nope] = (q[:, base:base + nope] * scale).astype(o_ref.dtype)
        r = _rope_slot(q[:, base + nope:base + slot], cos, sina, sinb, half)
        o_ref[:, base + nope:base + slot] = (r * scale).astype(o_ref.dtype)


def q_proj(c, w, tables, dims, scale):
    t, kdim = c.shape
    n = w.shape[1]
    slot = dims.q_head_pad
    bm = _pick(t, 1024, SUBLANES)
    bn = _pick(n, 4 * slot, slot)
    row = lambda i, j: (i, 0)
    vmem = 2 * (bm * kdim * 2 + kdim * bn * 2 + bm * bn * 2 + 3 * bm * LANES * 4) + 2 * bm * bn * 4
    return pl.pallas_call(
        functools.partial(_q_proj_kernel, half=dims.qk_rope // 2, nope=dims.qk_nope, slot=slot,
                          scale=scale),
        grid=(t // bm, n // bn),
        in_specs=[pl.BlockSpec((bm, kdim), row),
                  pl.BlockSpec((kdim, bn), lambda i, j: (0, j)),
                  pl.BlockSpec((bm, LANES), row),
                  pl.BlockSpec((bm, LANES), row),
                  pl.BlockSpec((bm, LANES), row)],
        out_specs=pl.BlockSpec((bm, bn), lambda i, j: (i, j)),
        out_shape=jax.ShapeDtypeStruct((t, n), BF16),
        compiler_params=_params(("parallel", "parallel"), vmem),
        name="q_proj",
    )(c, w, *tables)


def _kv_proj_kernel(c_ref, wk_ref, wv_ref, kr_ref, k_ref, v_ref, *, nope, slot):
    c = c_ref[...]
    kn = jnp.dot(c, wk_ref[...], preferred_element_type=F32)
    v_ref[...] = jnp.dot(c, wv_ref[...], preferred_element_type=F32).astype(v_ref.dtype)
    kr = kr_ref[...]
    for h in range(kn.shape[1] // nope):
        k_ref[:, h * slot:h * slot + nope] = kn[:, h * nope:(h + 1) * nope].astype(k_ref.dtype)
        k_ref[:, h * slot + nope:(h + 1) * slot] = kr


def kv_proj(c, wk, wv, kr, dims):
    t, kdim = c.shape
    nope, vh, slot = dims.qk_nope, dims.v_head, dims.q_head_pad
    hb = _pick(dims.heads, 8, 1)
    bm = _pick(t, 1024, SUBLANES)
    row = lambda i, j: (i, 0)
    col = lambda i, j: (0, j)
    out = lambda i, j: (i, j)
    vmem = 2 * (bm * kdim * 2 + kdim * hb * (nope + vh) * 2 + bm * LANES * 2
                + bm * hb * (slot + vh) * 2) + bm * hb * (nope + vh) * 4
    return pl.pallas_call(
        functools.partial(_kv_proj_kernel, nope=nope, slot=slot),
        grid=(t // bm, dims.heads // hb),
        in_specs=[pl.BlockSpec((bm, kdim), row),
                  pl.BlockSpec((kdim, hb * nope), col),
                  pl.BlockSpec((kdim, hb * vh), col),
                  pl.BlockSpec((bm, slot - nope), row)],
        out_specs=[pl.BlockSpec((bm, hb * slot), out),
                   pl.BlockSpec((bm, hb * vh), out)],
        out_shape=[jax.ShapeDtypeStruct((t, dims.heads * slot), BF16),
                   jax.ShapeDtypeStruct((t, dims.heads * vh), BF16)],
        compiler_params=_params(("parallel", "parallel"), vmem),
        name="kv_proj",
    )(c, wk, wv, kr)


def _attn_kernel(q_ref, k_ref, v_ref, o_ref, vx_ref, m_ref, acc_ref, *, blk, tk, chains, vh):
    qi = pl.program_id(2)
    rows = blk // chains

    @pl.when(qi == 0)
    def _():
        vx_ref[:, :vh] = v_ref[...]
        vx_ref[:, vh:] = jnp.ones((vx_ref.shape[0], vx_ref.shape[1] - vh), vx_ref.dtype)

    m_ref[...] = jnp.full(m_ref.shape, -jnp.inf, F32)
    acc_ref[...] = jnp.zeros(acc_ref.shape, F32)

    def chain(c, start, n, diagonal):
        r = slice(c * rows, (c + 1) * rows)
        k = k_ref[pl.ds(start, n), :]
        s = lax.dot_general(q_ref[r, :], k, (((1,), (1,)), ((), ())), preferred_element_type=F32)
        if diagonal:
            tail = s[:, n - rows:]
            row = lax.broadcasted_iota(jnp.int32, tail.shape, 0)
            col = lax.broadcasted_iota(jnp.int32, tail.shape, 1)
            tail = jnp.where(col <= row, tail, MASK_VALUE)
            s = tail if n == rows else jnp.concatenate([s[:, :n - rows], tail], axis=1)
        m_old = m_ref[r, :]
        m_new = jnp.maximum(m_old, jnp.max(s, axis=-1, keepdims=True))
        alpha = jnp.exp2(m_old - m_new)
        p = jnp.exp2(s - m_new).astype(vx_ref.dtype)
        pv = jnp.dot(p, vx_ref[pl.ds(start, n), :], preferred_element_type=F32)
        acc_ref[r, :] = alpha * acc_ref[r, :] + pv
        m_ref[r, :] = m_new

    def full_block(kb):
        start = pl.multiple_of(kb * tk, tk)
        for c in range(chains):
            chain(c, start, tk, False)

    def body(pair, carry):
        full_block(2 * pair)
        full_block(2 * pair + 1)
        return carry

    n_full = qi * (blk // tk)
    lax.fori_loop(0, n_full // 2, body, 0)

    def diagonal_block():
        start = pl.multiple_of(qi * blk, blk)
        for c in range(chains):
            chain(c, start, (c + 1) * rows, True)

    @pl.when(n_full % 2 == 1)
    def _():
        full_block(n_full - 1)
        diagonal_block()

    @pl.when(n_full % 2 == 0)
    def _():
        diagonal_block()

    acc = acc_ref[...]
    o_ref[...] = (acc[:, :vh] / acc[:, vh:]).astype(o_ref.dtype)


def attention(q, k, v, dims):
    b, s, h = dims.batch, dims.seq, dims.heads
    slot, vh = dims.q_head_pad, dims.v_head
    blk = _pick(s, 1024, 2 * LANES)
    chains = 4
    tk = blk
    nq = s // blk
    vmem = (2 * (blk * slot * 2 + s * slot * 2 + s * vh * 2 + blk * vh * 2) + s * 2 * vh * 2
            + blk * 2 * vh * 4 + 2 * blk * blk * 4)
    return pl.pallas_call(
        functools.partial(_attn_kernel, blk=blk, tk=tk, chains=chains, vh=vh),
        grid=(b, h, nq),
        in_specs=[pl.BlockSpec((blk, slot), lambda bi, hi, qi: (bi * nq + qi, hi)),
                  pl.BlockSpec((s, slot), lambda bi, hi, qi: (bi, hi)),
                  pl.BlockSpec((s, vh), lambda bi, hi, qi: (bi, hi))],
        out_specs=pl.BlockSpec((blk, vh), lambda bi, hi, qi: (bi * nq + qi, hi)),
        out_shape=jax.ShapeDtypeStruct((b * s, h * vh), BF16),
        scratch_shapes=[pltpu.VMEM((s, 2 * vh), BF16),
                        pltpu.VMEM((blk, 1), F32),
                        pltpu.VMEM((blk, 2 * vh), F32)],
        compiler_params=_params(("parallel", "parallel", "arbitrary"), vmem),
        name="attention",
    )(q, k, v)


def _rg_in_kernel(u_ref, wy_ref, wx_ref, y_ref, x_ref):
    u = u_ref[...]
    y = jnp.dot(u, wy_ref[...].astype(u.dtype), preferred_element_type=F32)
    y_ref[...] = jax.nn.gelu(y, approximate=True).astype(y_ref.dtype)
    x_ref[...] = jnp.dot(u, wx_ref[...].astype(u.dtype), preferred_element_type=F32)


def rg_in_proj(u, w_in):
    t, d = u.shape
    w = w_in.shape[1] // 2
    bm = _pick(t, 1024, SUBLANES)
    bn = _pick(w, 512, LANES)
    nj = w // bn
    wbytes = jnp.dtype(w_in.dtype).itemsize
    vmem = bm * d * 2 + 2 * (2 * d * bn * wbytes + bm * bn * 6) + 2 * d * bn * 2 + 3 * bm * bn * 4
    return pl.pallas_call(
        _rg_in_kernel,
        grid=(t // bm, nj),
        in_specs=[pl.BlockSpec((bm, d), lambda i, j: (i, 0), pipeline_mode=pl.Buffered(1)),
                  pl.BlockSpec((d, bn), lambda i, j: (0, j)),
                  pl.BlockSpec((d, bn), lambda i, j: (0, j + nj))],
        out_specs=[pl.BlockSpec((bm, bn), lambda i, j: (i, j)),
                   pl.BlockSpec((bm, bn), lambda i, j: (i, j))],
        out_shape=[jax.ShapeDtypeStruct((t, w), BF16), jax.ShapeDtypeStruct((t, w), F32)],
        compiler_params=_params(("parallel", "parallel"), vmem),
        name="rg_in_proj",
    )(u, w_in, w_in)


def _rglru_kernel(x_ref, y_ref, cw_ref, cb_ref, wa_ref, ba_ref, wx_ref, bx_ref, ap_ref, o_ref,
                  xe_ref, a_ref, b_ref, h_ref, *, tc, lru_block, conv_width):
    @pl.when(pl.program_id(2) == 0)
    def _():
        xe_ref[0:SUBLANES, :] = jnp.zeros((SUBLANES, xe_ref.shape[1]), F32)
        h_ref[...] = jnp.zeros(h_ref.shape, F32)

    x = x_ref[...]
    xe_ref[SUBLANES:, :] = x
    xc = x * cw_ref[conv_width - 1:conv_width, :] + cb_ref[...]
    for d in range(1, conv_width):
        xc = xc + xe_ref[SUBLANES - d:SUBLANES - d + tc, :] * cw_ref[conv_width - 1 - d:conv_width - d, :]
    xe_ref[0:SUBLANES, :] = x[tc - SUBLANES:, :]

    neg_c_softplus = -LRU_C * jax.nn.softplus(-ap_ref[...])
    for n in range(x.shape[1] // lru_block):
        sl = slice(n * lru_block, (n + 1) * lru_block)
        xb = xc[:, sl]
        xb16 = xb.astype(BF16)
        gate_r = jax.nn.sigmoid(jnp.dot(xb16, wa_ref[n], preferred_element_type=F32) + ba_ref[:, sl])
        gate_i = jax.nn.sigmoid(jnp.dot(xb16, wx_ref[n], preferred_element_type=F32) + bx_ref[:, sl])
        log_a = gate_r * neg_c_softplus[:, sl]
        a = jnp.exp(log_a)
        a_ref[:, sl] = a
        b_ref[:, sl] = jnp.sqrt(-jnp.tanh(log_a) * (a * a + 1.0)) * (gate_i * xb)

    row = lax.broadcasted_iota(jnp.int32, (SUBLANES, x.shape[1]), 0)

    def group(g, h_prev):
        start = pl.multiple_of(g * SUBLANES, SUBLANES)
        a = a_ref[pl.ds(start, SUBLANES), :]
        b = b_ref[pl.ds(start, SUBLANES), :]
        d = 1
        while d < SUBLANES:
            keep = row >= d
            b = jnp.where(keep, a * pltpu.roll(b, d, axis=0) + b, b)
            a = jnp.where(keep, a * pltpu.roll(a, d, axis=0), a)
            d *= 2
        h = a * h_prev + b
        b_ref[pl.ds(start, SUBLANES), :] = h
        return jnp.broadcast_to(h[SUBLANES - 1:SUBLANES, :], h.shape)

    h_ref[...] = lax.fori_loop(0, tc // SUBLANES, group, h_ref[...], unroll=4)
    o_ref[...] = (b_ref[...] * y_ref[...].astype(F32)).astype(o_ref.dtype)


def rglru_core(x, y, conv_w, conv_b, wa, ba, wx, bx, a_param, dims):
    b, s, w = dims.batch, dims.seq, dims.lru_width
    lb = dims.lru_block
    cw = _pick(w, 2 * lb, lb)
    tc = _pick(s, 512, SUBLANES)
    nt = s // tc
    npb = cw // lb
    tile = lambda bi, ci, ti: (bi * nt + ti, ci)
    chan = lambda bi, ci, ti: (0, ci)
    wblk = lambda bi, ci, ti: (ci, 0, 0)
    vmem = 2 * (tc * cw * (4 + 2 + 2) + 2 * npb * lb * lb * 2) + 8 * tc * cw * 4
    return pl.pallas_call(
        functools.partial(_rglru_kernel, tc=tc, lru_block=lb, conv_width=dims.conv_width),
        grid=(b, w // cw, nt),
        in_specs=[pl.BlockSpec((tc, cw), tile),
                  pl.BlockSpec((tc, cw), tile),
                  pl.BlockSpec((dims.conv_width, cw), chan),
                  pl.BlockSpec((1, cw), chan),
                  pl.BlockSpec((npb, lb, lb), wblk),
                  pl.BlockSpec((1, cw), chan),
                  pl.BlockSpec((npb, lb, lb), wblk),
                  pl.BlockSpec((1, cw), chan),
                  pl.BlockSpec((1, cw), chan)],
        out_specs=pl.BlockSpec((tc, cw), tile),
        out_shape=jax.ShapeDtypeStruct((b * s, w), BF16),
        scratch_shapes=[pltpu.VMEM((tc + SUBLANES, cw), F32),
                        pltpu.VMEM((tc, cw), F32),
                        pltpu.VMEM((tc, cw), F32),
                        pltpu.VMEM((SUBLANES, cw), F32)],
        compiler_params=_params(("parallel", "parallel", "arbitrary"), vmem),
        name="rglru_core",
    )(x, y, conv_w, conv_b.reshape(1, w), wa, ba.reshape(1, w), wx, bx.reshape(1, w),
      a_param.reshape(1, w))


def _ffn(h, g, w_in, w_out):
    act = swiglu_in(rmsnorm(h, g, BF16), w_in)
    return matmul_residual(act, w_out.astype(BF16), h, 0.5)


def _pad_heads(w, dims, width):
    k = w.shape[0]
    w = w.reshape(k, dims.heads, width)
    w = jnp.pad(w, ((0, 0), (0, 0), (0, dims.q_head_pad - width)))
    return w.reshape(k, dims.heads * dims.q_head_pad)


def _mla(h, g, tables, w_in, q_norm, kv_norm, w_uq, w_ukv, w_o, dims):
    u = rmsnorm(h, g, BF16)
    ql, kvl, rope = dims.q_lora, dims.kv_lora, dims.qk_rope
    slot = dims.q_head_pad
    w_in = w_in.astype(BF16)
    w_kr = jnp.pad(w_in[:, ql + kvl:], ((0, 0), (0, slot - dims.qk_nope - rope)))
    cq, ckv, kr = mla_in_proj(u, w_in[:, :ql], w_in[:, ql:ql + kvl], w_kr, q_norm, kv_norm, tables,
                              rope // 2)
    scale = (dims.qk_nope + rope) ** -0.5 * math.log2(math.e)
    q = q_proj(cq, _pad_heads(w_uq.astype(BF16), dims, dims.qk_nope + rope), tables, dims, scale)
    w_ukv = w_ukv.astype(BF16).reshape(kvl, dims.heads, dims.qk_nope + dims.v_head)
    w_k = w_ukv[:, :, :dims.qk_nope].reshape(kvl, dims.heads * dims.qk_nope)
    w_v = w_ukv[:, :, dims.qk_nope:].reshape(kvl, dims.heads * dims.v_head)
    k, v = kv_proj(ckv, w_k, w_v, kr, dims)
    o = attention(q, k, v, dims)
    return matmul_residual(o, w_o.astype(BF16), h, 1.0)


def _rglru(h, g, w_in, conv_w, conv_b, wa, ba, wx, bx, a_param, w_out, dims):
    u = rmsnorm(h, g, BF16)
    y, x = rg_in_proj(u, w_in)
    hy = rglru_core(x, y, conv_w, conv_b, wa.astype(BF16), ba, wx.astype(BF16), bx, a_param, dims)
    return matmul_residual(hy, w_out.astype(BF16), h, 1.0)


def _forward(dims, x, positions, norm_ffn1, ffn1_in, ffn1_out, norm_mix, norm_ffn2, ffn2_in, ffn2_out,
             mla_in, mla_q_norm, mla_kv_norm, mla_w_uq, mla_w_ukv, mla_w_o,
             rg_in, rg_conv_w, rg_conv_b, rg_gate_a_w, rg_gate_a_b, rg_gate_x_w, rg_gate_x_b,
             rg_a_param, rg_out, norm_final):
    depth = norm_ffn1.shape[0]
    h = x.reshape(dims.tokens, dims.d_model)
    tables = rope_tables(positions, dims.qk_rope)
    for i in range(depth):
        h = _ffn(h, norm_ffn1[i], ffn1_in[i], ffn1_out[i])
        j = i // 2
        if i % 2 == 0:
            h = _mla(h, norm_mix[i], tables, mla_in[j], mla_q_norm[j], mla_kv_norm[j], mla_w_uq[j],
                     mla_w_ukv[j], mla_w_o[j], dims)
        else:
            h = _rglru(h, norm_mix[i], rg_in[j], rg_conv_w[j], rg_conv_b[j], rg_gate_a_w[j],
                       rg_gate_a_b[j].reshape(-1), rg_gate_x_w[j], rg_gate_x_b[j].reshape(-1),
                       rg_a_param[j], rg_out[j], dims)
        h = _ffn(h, norm_ffn2[i], ffn2_in[i], ffn2_out[i])
    out = rmsnorm(h, norm_final, x.dtype)
    return out.reshape(x.shape)


def kernel(x, positions, norm_ffn1, ffn1_in, ffn1_out, norm_mix, norm_ffn2, ffn2_in, ffn2_out, mla_in, mla_q_norm, mla_kv_norm, mla_w_uq, mla_w_ukv, mla_w_o, rg_in, rg_conv_w, rg_conv_b, rg_gate_a_w, rg_gate_a_b, rg_gate_x_w, rg_gate_x_b, rg_a_param, rg_out, norm_final):
    batch, seq, d_model = x.shape
    heads = MLA_HEADS
    v_head = mla_w_o.shape[1] // heads
    qk_nope = mla_w_ukv.shape[2] // heads - v_head
    dims = Dims(
        batch=batch, seq=seq, d_model=d_model, d_ff=ffn1_out.shape[1], heads=heads,
        qk_nope=qk_nope, qk_rope=mla_w_uq.shape[2] // heads - qk_nope, v_head=v_head,
        q_lora=mla_q_norm.shape[1], kv_lora=mla_kv_norm.shape[1],
        lru_width=rg_a_param.shape[1], lru_blocks=rg_gate_a_w.shape[1],
        conv_width=rg_conv_w.shape[1])
    return _forward(dims, x, positions, norm_ffn1, ffn1_in, ffn1_out, norm_mix, norm_ffn2, ffn2_in,
                    ffn2_out, mla_in, mla_q_norm, mla_kv_norm, mla_w_uq, mla_w_ukv, mla_w_o,
                    rg_in, rg_conv_w, rg_conv_b, rg_gate_a_w, rg_gate_a_b, rg_gate_x_w, rg_gate_x_b,
                    rg_a_param, rg_out, norm_final)
```
